```python
import math
import jax, jax.numpy as jnp
from jax import lax
import numpy as np

D_MODEL = 1024
BATCH = 2
SEQ = 8192
DEPTH = 4
DEC_BATCH = 128
DEC_SEQ = 8
PAST_LEN = 2048
PAGE_SIZE = 128

HEAD_DIM = 64
SB_HEADS = 8
DA_HEADS = 4
SB_WIDTH = SB_HEADS * HEAD_DIM
DA_WIDTH = DA_HEADS * 2 * HEAD_DIM
MIX_WIDTH = SB_WIDTH + DA_WIDTH
IN_WIDTH = 4 * SB_WIDTH + 4 * DA_WIDTH
ROPE_THETA = 10000.0
EPS = 1e-6
Q_BLOCK = 128

kernel_name = "hymba_stickbreak_diffattn_decoder_step"


def rmsnorm(x, g):
    xf = x.astype(jnp.float32)
    y = xf * lax.rsqrt(jnp.mean(xf * xf, axis=-1, keepdims=True) + EPS)
    return (y * g.astype(jnp.float32)).astype(x.dtype)


def rope(x, pos):
    d = x.shape[-1]
    inv_freq = ROPE_THETA ** (-jnp.arange(0, d, 2, dtype=jnp.float32) / d)
    ang = pos.astype(jnp.float32)[:, None] * inv_freq[None, :]
    ang = jnp.concatenate([ang, ang], axis=-1)[:, None, :]
    xf = x.astype(jnp.float32)
    x1, x2 = jnp.split(xf, 2, axis=-1)
    rot = jnp.concatenate([-x2, x1], axis=-1)
    return (xf * jnp.cos(ang) + rot * jnp.sin(ang)).astype(x.dtype)


def stick_breaking_attend(q, k, v, q_pos, k_pos):
    z = jnp.einsum('bqhd,bkhd->bhqk', q, k).astype(jnp.float32) * (HEAD_DIM ** -0.5)
    causal = k_pos[None, :] < q_pos[:, None]
    sp = jnp.where(causal, jax.nn.softplus(z), 0.0)
    between = lax.cumsum(sp, axis=3, reverse=True) - sp
    a = jnp.where(causal, jnp.exp(jax.nn.log_sigmoid(z) - between), 0.0)
    return jnp.einsum('bhqk,bkhd->bqhd', a.astype(v.dtype), v)


def differential_attend(q, k, v, q_pos, k_pos, lam):
    s = jnp.einsum('bqhmd,bkhmd->bhmqk', q, k).astype(jnp.float32) * (HEAD_DIM ** -0.5)
    causal = k_pos[None, :] <= q_pos[:, None]
    p = jax.nn.softmax(jnp.where(causal, s, -jnp.inf), axis=-1)
    a = p[:, :, 0] - lam * p[:, :, 1]
    return jnp.einsum('bhqk,bkhe->bqhe', a.astype(v.dtype), v)


def attend_blocked(fn, q, q_pos):
    B, T = q.shape[0], q.shape[1]
    nb = T // Q_BLOCK
    qb = jnp.moveaxis(q.reshape((B, nb, Q_BLOCK) + q.shape[2:]), 1, 0)
    pb = q_pos.reshape(nb, Q_BLOCK)
    out = lax.map(lambda a: fn(a[0], a[1]), (qb, pb))
    out = jnp.moveaxis(out, 0, 1)
    return out.reshape((B, T) + out.shape[3:])


def gather_pages(cache, l, page_table):
    pages = cache[l][page_table]
    db, n_pages, ps = pages.shape[0], pages.shape[1], pages.shape[2]
    return pages.reshape((db, n_pages * ps) + pages.shape[3:])


def mixer_layer(x, c, q_pos, past, l, w_mod, b_mod, norm_g, w_in, w_out,
                lambda_q1, lambda_k1, lambda_q2, lambda_k2, subln_g, blocked):
    B, T = x.shape[0], x.shape[1]
    mod = jax.nn.silu(c) @ w_mod[l] + b_mod[l]
    shift, scale, gate = jnp.split(mod, 3, axis=-1)
    h = rmsnorm(x, norm_g[l]) * (1 + scale[:, None]) + shift[:, None]
    proj = h @ w_in[l]
    splits = [SB_WIDTH * (i + 1) for i in range(4)] + [4 * SB_WIDTH + DA_WIDTH * (i + 1) for i in range(3)]
    sb_q, sb_k, sb_v, sb_g, da_q, da_k, da_v, da_g = jnp.split(proj, splits, axis=-1)
    sb_q = sb_q.reshape(B, T, SB_HEADS, HEAD_DIM)
    sb_k = sb_k.reshape(B, T, SB_HEADS, HEAD_DIM)
    sb_v = sb_v.reshape(B, T, SB_HEADS, HEAD_DIM)
    da_q = rope(da_q.reshape(B, T, 2 * DA_HEADS, HEAD_DIM), q_pos).reshape(B, T, DA_HEADS, 2, HEAD_DIM)
    da_k = rope(da_k.reshape(B, T, 2 * DA_HEADS, HEAD_DIM), q_pos).reshape(B, T, DA_HEADS, 2 * HEAD_DIM)
    da_v = da_v.reshape(B, T, DA_HEADS, 2 * HEAD_DIM)

    if past is None:
        k_sb, v_sb, k_da, v_da = sb_k, sb_v, da_k, da_v
        k_pos = q_pos
    else:
        p_sbk, p_sbv, p_dak, p_dav = past
        k_sb = jnp.concatenate([p_sbk, sb_k], axis=1)
        v_sb = jnp.concatenate([p_sbv, sb_v], axis=1)
        k_da = jnp.concatenate([p_dak, da_k], axis=1)
        v_da = jnp.concatenate([p_dav, da_v], axis=1)
        k_pos = jnp.concatenate([jnp.arange(p_sbk.shape[1], dtype=q_pos.dtype), q_pos])
    k_da4 = k_da.reshape(k_da.shape[0], k_da.shape[1], DA_HEADS, 2, HEAD_DIM)

    lam_init = 0.8 - 0.6 * math.exp(-0.3 * l)
    lam = (jnp.exp(jnp.sum(lambda_q1[l].astype(jnp.float32) * lambda_k1[l].astype(jnp.float32)))
           - jnp.exp(jnp.sum(lambda_q2[l].astype(jnp.float32) * lambda_k2[l].astype(jnp.float32)))
           + lam_init)

    sb_fn = lambda qq, qp: stick_breaking_attend(qq, k_sb, v_sb, qp, k_pos)
    da_fn = lambda qq, qp: differential_attend(qq, k_da4, v_da, qp, k_pos, lam)
    if blocked:
        o_sb = attend_blocked(sb_fn, sb_q, q_pos)
        o_da = attend_blocked(da_fn, da_q, q_pos)
    else:
        o_sb = sb_fn(sb_q, q_pos)
        o_da = da_fn(da_q, q_pos)
    o_da = rmsnorm(o_da, subln_g[l]) * (1 - lam_init)

    mixed = jnp.concatenate([o_sb.reshape(B, T, SB_WIDTH) * jax.nn.silu(sb_g),
                             o_da.reshape(B, T, DA_WIDTH) * jax.nn.silu(da_g)], axis=-1)
    x = x + gate[:, None] * (mixed @ w_out[l])
    return x, (sb_k, sb_v, da_k, da_v)


def setup_inputs(seed: int = 0) -> dict:
    key = jax.random.key(seed)
    ks = jax.random.split(key, 24)
    f32 = jnp.float32
    n_pages = PAST_LEN // PAGE_SIZE
    n_used = DEC_BATCH * n_pages
    n_pool = n_used + n_used // 4

    def nrm(k, shape, s=1.0):
        return s * jax.random.normal(k, shape, f32)

    page_table = jax.random.permutation(ks[8], n_pool)[:n_used].reshape(DEC_BATCH, n_pages).astype(jnp.int32)
    return {
        "x_prompt": nrm(ks[0], (BATCH, SEQ, D_MODEL)),
        "x_sample": nrm(ks[1], (DEC_BATCH, DEC_SEQ, D_MODEL)),
        "c_prompt": nrm(ks[2], (BATCH, D_MODEL)),
        "c_sample": nrm(ks[3], (DEC_BATCH, D_MODEL)),
        "cache_sb_k": nrm(ks[4], (DEPTH, n_pool, PAGE_SIZE, SB_HEADS, HEAD_DIM)),
        "cache_sb_v": nrm(ks[5], (DEPTH, n_pool, PAGE_SIZE, SB_HEADS, HEAD_DIM)),
        "cache_da_k": nrm(ks[6], (DEPTH, n_pool, PAGE_SIZE, DA_HEADS, 2 * HEAD_DIM)),
        "cache_da_v": nrm(ks[7], (DEPTH, n_pool, PAGE_SIZE, DA_HEADS, 2 * HEAD_DIM)),
        "page_table": page_table,
        "w_mod": nrm(ks[9], (DEPTH, D_MODEL, 3 * D_MODEL), 0.5 * D_MODEL ** -0.5),
        "b_mod": nrm(ks[10], (DEPTH, 3 * D_MODEL), 0.01),
        "norm_g": 1.0 + nrm(ks[11], (DEPTH, D_MODEL), 0.02),
        "w_in": nrm(ks[12], (DEPTH, D_MODEL, IN_WIDTH), D_MODEL ** -0.5),
        "w_out": nrm(ks[13], (DEPTH, MIX_WIDTH, D_MODEL), MIX_WIDTH ** -0.5),
        "lambda_q1": nrm(ks[14], (DEPTH, HEAD_DIM), 0.1),
        "lambda_k1": nrm(ks[15], (DEPTH, HEAD_DIM), 0.1),
        "lambda_q2": nrm(ks[16], (DEPTH, HEAD_DIM), 0.1),
        "lambda_k2": nrm(ks[17], (DEPTH, HEAD_DIM), 0.1),
        "subln_g": 1.0 + nrm(ks[18], (DEPTH, 2 * HEAD_DIM), 0.02),
        "final_norm_g": 1.0 + nrm(ks[19], (D_MODEL,), 0.02),
    }


def reference(x_prompt, x_sample, c_prompt, c_sample, cache_sb_k, cache_sb_v, cache_da_k, cache_da_v,
              page_table, w_mod, b_mod, norm_g, w_in, w_out, lambda_q1, lambda_k1, lambda_q2, lambda_k2,
              subln_g, final_norm_g):
    weights = (w_mod, b_mod, norm_g, w_in, w_out, lambda_q1, lambda_k1, lambda_q2, lambda_k2, subln_g)
    pos_p = jnp.arange(x_prompt.shape[1], dtype=jnp.int32)
    past_len = page_table.shape[1] * cache_sb_k.shape[2]
    pos_s = past_len + jnp.arange(x_sample.shape[1], dtype=jnp.int32)
    hp, hs = x_prompt, x_sample
    rows_p, rows_s = [], []
    for l in range(DEPTH):
        hp, new_p = mixer_layer(hp, c_prompt, pos_p, None, l, *weights, blocked=True)
        past = (gather_pages(cache_sb_k, l, page_table), gather_pages(cache_sb_v, l, page_table),
                gather_pages(cache_da_k, l, page_table), gather_pages(cache_da_v, l, page_table))
        hs, new_s = mixer_layer(hs, c_sample, pos_s, past, l, *weights, blocked=False)
        rows_p.append(new_p)
        rows_s.append(new_s)
    y_prompt = rmsnorm(hp, final_norm_g)
    y_sample = rmsnorm(hs, final_norm_g)
    sb_k_p, sb_v_p, da_k_p, da_v_p = [jnp.stack(r, axis=0) for r in zip(*rows_p)]
    sb_k_s, sb_v_s, da_k_s, da_v_s = [jnp.stack(r, axis=0) for r in zip(*rows_s)]
    return (y_prompt, y_sample, sb_k_p, sb_v_p, da_k_p, da_v_p, sb_k_s, sb_v_s, da_k_s, da_v_s)
```

```python
import functools
import math

import jax
import jax.numpy as jnp
from jax import lax
from jax.experimental import pallas as pl
from jax.experimental.pallas import tpu as pltpu

F32 = jnp.float32
BF16 = jnp.bfloat16

HEAD_DIM = 64
LANES = 128
SLABS = 4
GROUP_WIDTH = SLABS * LANES
ROPE_THETA = 10000.0
EPS = 1e-6
Q_SCALE = HEAD_DIM ** -0.5
EXP_ZERO_BELOW = -104.0
VMEM_LIMIT = 48 * 1024 * 1024


def _cparams(sem):
    return pltpu.CompilerParams(dimension_semantics=sem, vmem_limit_bytes=VMEM_LIMIT)


def _mod_kernel(c_ref, w_ref, b_ref, o_ref):
    c = c_ref[...]
    sc = c / (1.0 + jnp.exp(-c))
    o_ref[0, 0] = jnp.dot(sc, w_ref[0], precision=lax.Precision.HIGHEST,
                          preferred_element_type=F32) + b_ref[0, 0]


def _modulation(c_all, w_mod, b_mod):
    depth, d, _ = w_mod.shape
    r = c_all.shape[0]
    return pl.pallas_call(
        _mod_kernel,
        grid=(depth, 3),
        in_specs=[
            pl.BlockSpec((r, d), lambda l, j: (0, 0)),
            pl.BlockSpec((1, d, d), lambda l, j: (l, 0, j)),
            pl.BlockSpec((1, 1, 1, d), lambda l, j: (l, j, 0, 0)),
        ],
        out_specs=pl.BlockSpec((1, 1, r, d), lambda l, j: (l, j, 0, 0)),
        out_shape=jax.ShapeDtypeStruct((depth, 3, r, d), F32),
        compiler_params=_cparams(("arbitrary", "arbitrary")),
        name="modulation",
    )(c_all, w_mod, b_mod.reshape(depth, 3, 1, d))


def _inproj_kernel(x_ref, scale_ref, shift_ref, g_ref, w_ref, cos_ref, sin_up_ref, sin_dn_ref,
                   sbq_ref, sbk_ref, sbv_ref, sbg_ref, daq_ref, dak_ref, dav_ref, dag_ref,
                   sbk16_ref, sbv16_ref, dak16_ref, dav16_ref):
    x = x_ref[...]
    xn = x * lax.rsqrt(jnp.mean(x * x, axis=-1, keepdims=True) + EPS)
    h = (xn * g_ref[...]) * (1.0 + scale_ref[...]) + shift_ref[...]
    h16 = h.astype(BF16)

    def seg(i):
        return jnp.dot(h16, w_ref[:, i * GROUP_WIDTH:(i + 1) * GROUP_WIDTH],
                       preferred_element_type=F32)

    def rope_slab(p):
        return (p * cos_ref[...] + pltpu.roll(p, LANES - HEAD_DIM // 2, 1) * sin_up_ref[...]
                + pltpu.roll(p, HEAD_DIM // 2, 1) * sin_dn_ref[...])

    sbq_ref[...] = (seg(0) * Q_SCALE).astype(BF16)
    k = seg(1)
    sbk_ref[...] = k
    sbk16_ref[...] = k.astype(BF16)
    v = seg(2)
    sbv_ref[...] = v
    sbv16_ref[...] = v.astype(BF16)
    g = seg(3)
    sbg_ref[...] = g / (1.0 + jnp.exp(-g))
    q = seg(4)
    for s in range(SLABS):
        sl = slice(s * LANES, (s + 1) * LANES)
        daq_ref[:, sl] = (rope_slab(q[:, sl]) * Q_SCALE).astype(BF16)
    k = seg(5)
    for s in range(SLABS):
        sl = slice(s * LANES, (s + 1) * LANES)
        kr = rope_slab(k[:, sl])
        dak_ref[:, sl] = kr
        dak16_ref[:, sl] = kr.astype(BF16)
    v = seg(6)
    dav_ref[...] = v
    dav16_ref[...] = v.astype(BF16)
    g = seg(7)
    dag_ref[...] = g / (1.0 + jnp.exp(-g))


def _inproj(x, scale, shift, norm_g, w16, cos_t, sin_up, sin_dn, *, tm, rows_per_group):
    rows, d = x.shape
    nblk = rows // tm
    if rows_per_group is None:
        mod_spec = pl.BlockSpec((tm, d), lambda i: (i, 0))
    else:
        per = rows_per_group // tm
        mod_spec = pl.BlockSpec((None, 1, d), lambda i: (i // per, 0, 0))
    tab_blocks = cos_t.shape[0] // tm
    tab_spec = pl.BlockSpec((tm, LANES), lambda i: (i % tab_blocks, 0))
    out_spec = pl.BlockSpec((tm, GROUP_WIDTH), lambda i: (i, 0))
    f32o = jax.ShapeDtypeStruct((rows, GROUP_WIDTH), F32)
    b16o = jax.ShapeDtypeStruct((rows, GROUP_WIDTH), BF16)
    return pl.pallas_call(
        _inproj_kernel,
        grid=(nblk,),
        in_specs=[
            pl.BlockSpec((tm, d), lambda i: (i, 0)),
            mod_spec, mod_spec,
            pl.BlockSpec((1, d), lambda i: (0, 0)),
            pl.BlockSpec(w16.shape, lambda i: (0, 0)),
            tab_spec, tab_spec, tab_spec,
        ],
        out_specs=[out_spec] * 12,
        out_shape=[b16o, f32o, f32o, f32o, b16o, f32o, f32o, f32o, b16o, b16o, b16o, b16o],
        compiler_params=_cparams(("arbitrary",)),
        name="inproj",
    )(x, scale, shift, norm_g, w16, cos_t, sin_up, sin_dn)


def _stack_queries(q):
    lane = lax.broadcasted_iota(jnp.int32, q.shape, 1)
    zero = jnp.zeros_like(q)
    return jnp.concatenate([jnp.where(lane < HEAD_DIM, q, zero),
                            jnp.where(lane < HEAD_DIM, zero, q)], axis=0)


def _scores(q2, k):
    return lax.dot_general(q2, k, (((1,), (1,)), ((), ())), preferred_element_type=F32)


def _suffix_matrix(tk):
    j = lax.broadcasted_iota(jnp.int32, (tk, tk), 0)
    s = lax.broadcasted_iota(jnp.int32, (tk, tk), 1)
    return jnp.where(j > s, 1.0, 0.0).astype(BF16)


def _sb_block(q2, k, v, run, valid, upper):
    z = _scores(q2, k)
    e = jnp.log(1.0 + jnp.exp(-jnp.abs(z)))
    sp = jnp.maximum(z, 0.0) + e
    ls = jnp.minimum(z, 0.0) - e
    if valid is not None:
        sp = jnp.where(valid, sp, 0.0)
    sp_hi = sp.astype(BF16)
    sp_lo = (sp - sp_hi.astype(F32)).astype(BF16)
    after = (jnp.dot(sp_hi, upper, preferred_element_type=F32)
             + jnp.dot(sp_lo, upper, preferred_element_type=F32))
    a = jnp.exp(ls - (after + run))
    if valid is not None:
        a = jnp.where(valid, a, 0.0)
    pv = jnp.dot(a.astype(BF16), v, preferred_element_type=F32)
    return pv, run + jnp.sum(sp, axis=-1, keepdims=True)


def _da_block(q2, k, v, m, l, acc, valid):
    s = _scores(q2, k)
    if valid is not None:
        s = jnp.where(valid, s, -jnp.inf)
    m_new = jnp.maximum(m, jnp.max(s, axis=-1, keepdims=True))
    alpha = jnp.exp(m - m_new)
    p = jnp.exp(s - m_new)
    l_new = alpha * l + jnp.sum(p, axis=-1, keepdims=True)
    acc_new = alpha * acc + jnp.dot(p.astype(BF16), v, preferred_element_type=F32)
    return m_new, l_new, acc_new


def _sb_finish(acc2, gate):
    tq = gate.shape[0]
    lane = lax.broadcasted_iota(jnp.int32, gate.shape, 1)
    o = jnp.where(lane < HEAD_DIM, acc2[:tq], acc2[tq:])
    return (o * gate).astype(BF16)


def _lambda(lam_ref, lam_init):
    lam = lam_ref[...]
    s1 = jnp.sum(lam[0:1] * lam[1:2], axis=-1, keepdims=True)
    s2 = jnp.sum(lam[2:3] * lam[3:4], axis=-1, keepdims=True)
    return jnp.exp(s1) - jnp.exp(s2) + lam_init


def _da_finish(acc2, l2, lam, subln_g, lam_init, gate):
    tq = gate.shape[0]
    o2 = acc2 / l2
    o = o2[:tq] - lam * o2[tq:]
    on = o * lax.rsqrt(jnp.mean(o * o, axis=-1, keepdims=True) + EPS)
    return ((on * subln_g) * (1.0 - lam_init) * gate).astype(BF16)


def _row_col(m, tk, tq):
    row = lax.broadcasted_iota(jnp.int32, (m, tk), 0)
    col = lax.broadcasted_iota(jnp.int32, (m, tk), 1)
    return jnp.where(row >= tq, row - tq, row), col


def _sb_prompt_kernel(q_ref, k_ref, v_ref, g_ref, o_ref, *, blk):
    i = pl.program_id(2)
    q2 = _stack_queries(q_ref[0])
    upper = _suffix_matrix(blk)
    t, s = _row_col(2 * blk, blk, blk)

    def kv(j):
        start = pl.multiple_of(j * blk, blk)
        return k_ref[0, pl.ds(start, blk), :], v_ref[0, pl.ds(start, blk), :]

    k, v = kv(i)
    run0 = jnp.zeros((2 * blk, 1), F32)
    acc, run = _sb_block(q2, k, v, run0, s < t, upper)

    def cond(c):
        j, _, run = c
        return jnp.logical_and(j >= 0, jnp.min(run) < -EXP_ZERO_BELOW)

    def body(c):
        j, acc, run = c
        k, v = kv(j)
        pv, run = _sb_block(q2, k, v, run, None, upper)
        return j - 1, acc + pv, run

    _, acc, _ = lax.while_loop(cond, body, (i - 1, acc, run))
    o_ref[0] = _sb_finish(acc, g_ref[0])


def _da_prompt_kernel(q_ref, k_ref, v_ref, g_ref, lam_ref, subln_ref, o_ref, *, blk, lam_init):
    i = pl.program_id(2)
    q2 = _stack_queries(q_ref[0])
    m2 = 2 * blk

    def kv(j):
        start = pl.multiple_of(j * blk, blk)
        return k_ref[0, pl.ds(start, blk), :], v_ref[0, pl.ds(start, blk), :]

    def body(j, c):
        k, v = kv(j)
        return _da_block(q2, k, v, *c, None)

    init = (jnp.full((m2, 1), -jnp.inf, F32), jnp.zeros((m2, 1), F32), jnp.zeros((m2, LANES), F32))
    c = lax.fori_loop(0, i, body, init)
    t, s = _row_col(m2, blk, blk)
    k, v = kv(i)
    _, l, acc = _da_block(q2, k, v, *c, s <= t)
    o_ref[0] = _da_finish(acc, l, _lambda(lam_ref, lam_init), subln_ref[...], lam_init, g_ref[0])


def _prompt_attention(kernel, q, k, v, g, extra, *, blk):
    b, t, _ = q.shape
    qspec = pl.BlockSpec((1, blk, LANES), lambda bi, s, i: (bi, i, s))
    kvspec = pl.BlockSpec((1, t, LANES), lambda bi, s, i: (bi, 0, s))
    extra_specs = [pl.BlockSpec(e.shape, lambda bi, s, i: (0, 0)) for e in extra]
    return pl.pallas_call(
        kernel,
        grid=(b, SLABS, t // blk),
        in_specs=[qspec, kvspec, kvspec, qspec] + extra_specs,
        out_specs=qspec,
        out_shape=jax.ShapeDtypeStruct((b, t, GROUP_WIDTH), BF16),
        compiler_params=_cparams(("arbitrary", "arbitrary", "arbitrary")),
        name=kernel.func.__name__.strip("_"),
    )(q, k, v, g, *extra)


def _sample_kernel(pt_ref, sbq_ref, sbkn_ref, sbvn_ref, sbg_ref, daq_ref, dakn_ref, davn_ref, dag_ref,
                   sbkp_ref, sbvp_ref, dakp_ref, davp_ref, lam_ref, subln_ref,
                   osb_ref, oda_ref, sb_acc, sb_run, da_m, da_l, da_acc, *, page, tq, lam_init):
    del pt_ref
    step = pl.program_id(1)
    last = pl.num_programs(1) - 1
    m2 = 2 * tq
    upper = _suffix_matrix(page)
    t, s = _row_col(m2, page, tq)

    def pad_rows(x):
        return jnp.concatenate([x, jnp.zeros((page - tq, x.shape[1]), x.dtype)], axis=0)

    @pl.when(step == 0)
    def _():
        sbk, sbv = pad_rows(sbkn_ref[0]), pad_rows(sbvn_ref[0])
        dak, dav = pad_rows(dakn_ref[0]), pad_rows(davn_ref[0])
        for sl in range(SLABS):
            ls = slice(sl * LANES, (sl + 1) * LANES)
            pv, run = _sb_block(_stack_queries(sbq_ref[0, :, ls]), sbk[:, ls], sbv[:, ls],
                                jnp.zeros((m2, 1), F32), s < t, upper)
            sb_acc[sl] = pv
            sb_run[sl] = run
            init = (jnp.full((m2, 1), -jnp.inf, F32), jnp.zeros((m2, 1), F32), jnp.zeros((m2, LANES), F32))
            m, l, acc = _da_block(_stack_queries(daq_ref[0, :, ls]), dak[:, ls], dav[:, ls], *init, s <= t)
            da_m[sl] = m
            da_l[sl] = l
            da_acc[sl] = acc

    @pl.when(step > 0)
    def _():
        sbk, sbv = sbkp_ref[0, 0].astype(BF16), sbvp_ref[0, 0].astype(BF16)
        dak, dav = dakp_ref[0, 0].astype(BF16), davp_ref[0, 0].astype(BF16)
        for sl in range(SLABS):
            ls = slice(sl * LANES, (sl + 1) * LANES)
            pv, run = _sb_block(_stack_queries(sbq_ref[0, :, ls]), sbk[:, ls], sbv[:, ls],
                                sb_run[sl], None, upper)
            sb_acc[sl] += pv
            sb_run[sl] = run
            m, l, acc = _da_block(_stack_queries(daq_ref[0, :, ls]), dak[:, ls], dav[:, ls],
                                  da_m[sl], da_l[sl], da_acc[sl], None)
            da_m[sl] = m
            da_l[sl] = l
            da_acc[sl] = acc

    @pl.when(step == last)
    def _():
        lam = _lambda(lam_ref, lam_init)
        for sl in range(SLABS):
            ls = slice(sl * LANES, (sl + 1) * LANES)
            osb_ref[0, :, ls] = _sb_finish(sb_acc[sl], sbg_ref[0, :, ls])
            oda_ref[0, :, ls] = _da_finish(da_acc[sl], da_l[sl], lam, subln_ref[...], lam_init,
                                           dag_ref[0, :, ls])


def _sample_attention(page_table, new, caches, layer, lam_vecs, subln_g, *, lam_init):
    sbq, sbk, sbv, sbg, daq, dak, dav, dag = new
    db, tq, _ = sbq.shape
    n_pages = page_table.shape[1]
    page = caches[0].shape[2]
    new_spec = pl.BlockSpec((1, tq, GROUP_WIDTH), lambda b, st, pt: (b, 0, 0))

    def page_index(b, st, pt):
        return (layer, pt[b, n_pages - jnp.maximum(st, 1)], 0, 0)

    page_spec = pl.BlockSpec((1, 1, page, GROUP_WIDTH), page_index)
    small = lambda e: pl.BlockSpec(e.shape, lambda b, st, pt: (0, 0))
    m2 = 2 * tq
    out = jax.ShapeDtypeStruct((db, tq, GROUP_WIDTH), BF16)
    grid_spec = pltpu.PrefetchScalarGridSpec(
        num_scalar_prefetch=1,
        grid=(db, n_pages + 1),
        in_specs=[new_spec] * 8 + [page_spec] * 4 + [small(lam_vecs), small(subln_g)],
        out_specs=[new_spec, new_spec],
        scratch_shapes=[
            pltpu.VMEM((SLABS, m2, LANES), F32), pltpu.VMEM((SLABS, m2, 1), F32),
            pltpu.VMEM((SLABS, m2, 1), F32), pltpu.VMEM((SLABS, m2, 1), F32),
            pltpu.VMEM((SLABS, m2, LANES), F32),
        ],
    )
    return pl.pallas_call(
        functools.partial(_sample_kernel, page=page, tq=tq, lam_init=lam_init),
        grid_spec=grid_spec,
        out_shape=[out, out],
        compiler_params=_cparams(("arbitrary", "arbitrary")),
        name="sample_attention",
    )(page_table, sbq, sbk, sbv, sbg, daq, dak, dav, dag, *caches, lam_vecs, subln_g)


def _outproj_kernel(x_ref, msb_ref, mda_ref, w_ref, gate_ref, fg_ref, o_ref, *, final):
    half = msb_ref.shape[1]
    y = (jnp.dot(msb_ref[...], w_ref[:half], preferred_element_type=F32)
         + jnp.dot(mda_ref[...], w_ref[half:], preferred_element_type=F32))
    y = x_ref[...] + gate_ref[...] * y
    if final:
        y = y * lax.rsqrt(jnp.mean(y * y, axis=-1, keepdims=True) + EPS) * fg_ref[...]
    o_ref[...] = y


def _outproj(x, m_sb, m_da, w16, gate, final_g, *, tm, rows_per_group, final):
    rows, d = x.shape
    if rows_per_group is None:
        gate_spec = pl.BlockSpec((tm, d), lambda i: (i, 0))
    else:
        per = rows_per_group // tm
        gate_spec = pl.BlockSpec((None, 1, d), lambda i: (i // per, 0, 0))
    row_spec = pl.BlockSpec((tm, d), lambda i: (i, 0))
    mix_spec = pl.BlockSpec((tm, GROUP_WIDTH), lambda i: (i, 0))
    return pl.pallas_call(
        functools.partial(_outproj_kernel, final=final),
        grid=(rows // tm,),
        in_specs=[row_spec, mix_spec, mix_spec, pl.BlockSpec(w16.shape, lambda i: (0, 0)),
                  gate_spec, pl.BlockSpec((1, d), lambda i: (0, 0))],
        out_specs=row_spec,
        out_shape=jax.ShapeDtypeStruct((rows, d), F32),
        compiler_params=_cparams(("arbitrary",)),
        name="outproj",
    )(x, m_sb, m_da, w16, gate, final_g)


def _rope_tables(pos):
    inv_freq = ROPE_THETA ** (-jnp.arange(0, HEAD_DIM, 2, dtype=F32) / HEAD_DIM)
    ang = pos.astype(F32)[:, None] * inv_freq[None, :]
    ang = jnp.concatenate([ang, ang, ang, ang], axis=-1)
    first_half = (jnp.arange(LANES) % HEAD_DIM) < HEAD_DIM // 2
    sin = jnp.sin(ang)
    return jnp.cos(ang), jnp.where(first_half, -sin, 0.0), jnp.where(first_half, 0.0, sin)


def _pick_tile(rows, target):
    tm = min(rows, target)
    while rows % tm:
        tm //= 2
    return tm


def kernel(x_prompt, x_sample, c_prompt, c_sample, cache_sb_k, cache_sb_v, cache_da_k, cache_da_v,
           page_table, w_mod, b_mod, norm_g, w_in, w_out, lambda_q1, lambda_k1, lambda_q2, lambda_k2,
           subln_g, final_norm_g):
    bsz, seq, d = x_prompt.shape
    db, dseq, _ = x_sample.shape
    depth = w_in.shape[0]
    n_pages, page = page_table.shape[1], cache_sb_k.shape[2]
    past_len = n_pages * page

    n_c = bsz + db
    c_all = jnp.concatenate([c_prompt, c_sample, jnp.zeros((-n_c % 8, d), F32)], axis=0)
    mod = _modulation(c_all, w_mod, b_mod)

    tabs_p = _rope_tables(jnp.arange(seq, dtype=jnp.int32))
    tabs_s = tuple(jnp.tile(t, (db, 1))
                   for t in _rope_tables(past_len + jnp.arange(dseq, dtype=jnp.int32)))

    caches = tuple(c.reshape(c.shape[0], c.shape[1], page, GROUP_WIDTH)
                   for c in (cache_sb_k, cache_sb_v, cache_da_k, cache_da_v))
    w_in16 = w_in.astype(BF16)
    w_out16 = w_out.astype(BF16)
    lam_all = jnp.stack([lambda_q1, lambda_k1, lambda_q2, lambda_k2], axis=1)
    fg = final_norm_g.reshape(1, d)

    rows_p, rows_s = bsz * seq, db * dseq
    tm_p = _pick_tile(seq, 256)
    tm_s = _pick_tile(rows_s, 256)
    blk = _pick_tile(seq, 256)

    hp = x_prompt.reshape(rows_p, d)
    hs = x_sample.reshape(rows_s, d)
    new_p, new_s = [], []
    for l in range(depth):
        lam_init = 0.8 - 0.6 * math.exp(-0.3 * l)
        g_l = norm_g[l].reshape(1, d)
        subln = subln_g[l].reshape(1, LANES)
        final = l == depth - 1

        def per_row(v):
            return jnp.broadcast_to(v[:, None, :], (db, dseq, d)).reshape(rows_s, d)

        shift_p, scale_p, gate_p = (mod[l, j, :bsz].reshape(bsz, 1, d) for j in range(3))
        shift_s, scale_s, gate_s = (per_row(mod[l, j, bsz:n_c]) for j in range(3))

        (sbq, sbk, sbv, sbg, daq, dak, dav, dag, sbk16, sbv16, dak16, dav16) = _inproj(
            hp, scale_p, shift_p, g_l, w_in16[l], *tabs_p, tm=tm_p, rows_per_group=seq)
        new_p.append((sbk, sbv, dak, dav))
        r3 = lambda a: a.reshape(bsz, seq, GROUP_WIDTH)
        m_sb = _prompt_attention(functools.partial(_sb_prompt_kernel, blk=blk),
                                 r3(sbq), r3(sbk16), r3(sbv16), r3(sbg), (), blk=blk)
        m_da = _prompt_attention(functools.partial(_da_prompt_kernel, blk=blk, lam_init=lam_init),
                                 r3(daq), r3(dak16), r3(dav16), r3(dag), (lam_all[l], subln), blk=blk)
        hp = _outproj(hp, m_sb.reshape(rows_p, GROUP_WIDTH), m_da.reshape(rows_p, GROUP_WIDTH),
                      w_out16[l], gate_p, fg, tm=tm_p, rows_per_group=seq, final=final)

        (sbq, sbk, sbv, sbg, daq, dak, dav, dag, sbk16, sbv16, dak16, dav16) = _inproj(
            hs, scale_s, shift_s, g_l, w_in16[l], *tabs_s, tm=tm_s, rows_per_group=None)
        new_s.append((sbk, sbv, dak, dav))
        s3 = lambda a: a.reshape(db, dseq, GROUP_WIDTH)
        m_sb, m_da = _sample_attention(
            page_table, tuple(s3(a) for a in (sbq, sbk16, sbv16, sbg, daq, dak16, dav16, dag)),
            caches, l, lam_all[l], subln, lam_init=lam_init)
        hs = _outproj(hs, m_sb.reshape(rows_s, GROUP_WIDTH), m_da.reshape(rows_s, GROUP_WIDTH),
                      w_out16[l], gate_s, fg, tm=tm_s, rows_per_group=None, final=final)

    def stack(rows, idx, lead, heads):
        return jnp.stack([r[idx] for r in rows], axis=0).reshape((depth,) + lead + heads)

    sb_heads = (GROUP_WIDTH // HEAD_DIM, HEAD_DIM)
    da_heads = (GROUP_WIDTH // (2 * HEAD_DIM), 2 * HEAD_DIM)
    outs_p = [stack(new_p, i, (bsz, seq), sb_heads if i < 2 else da_heads) for i in range(4)]
    outs_s = [stack(new_s, i, (db, dseq), sb_heads if i < 2 else da_heads) for i in range(4)]
    return (hp.reshape(bsz, seq, d), hs.reshape(db, dseq, d), *outs_p, *outs_s)
```

```python
import functools
import math

import jax
import jax.numpy as jnp
from jax import lax
from jax.experimental import pallas as pl
from jax.experimental.pallas import tpu as pltpu

F32 = jnp.float32
BF16 = jnp.bfloat16

HEAD_DIM = 64
LANES = 128
SLABS = 4
GROUP_WIDTH = SLABS * LANES
SB_HEADS = GROUP_WIDTH // HEAD_DIM
DA_HEADS = GROUP_WIDTH // (2 * HEAD_DIM)
ROPE_THETA = 10000.0
EPS = 1e-6
Q_SCALE = HEAD_DIM ** -0.5
LOG2E = math.log2(math.e)
EXP_ZERO_ABOVE = 104.0
VMEM_LIMIT = 48 * 1024 * 1024
SB_BLOCK = 256
DA_BLOCK = 512
PAGES_PER_STEP = 8
SB_PAGE_GROUP = 2


def _cparams(sem):
    return pltpu.CompilerParams(dimension_semantics=sem, vmem_limit_bytes=VMEM_LIMIT)


def _mod_kernel(c_ref, w_ref, b_ref, o_ref):
    c = c_ref[...]
    sc = c / (1.0 + jnp.exp(-c))
    o_ref[0, 0] = jnp.dot(sc, w_ref[0], precision=lax.Precision.HIGHEST,
                          preferred_element_type=F32) + b_ref[0, 0]


def _modulation(c_all, w_mod, b_mod):
    depth, d, _ = w_mod.shape
    r = c_all.shape[0]
    return pl.pallas_call(
        _mod_kernel,
        grid=(depth, 3),
        in_specs=[
            pl.BlockSpec((r, d), lambda l, j: (0, 0)),
            pl.BlockSpec((1, d, d), lambda l, j: (l, 0, j)),
            pl.BlockSpec((1, 1, 1, d), lambda l, j: (l, j, 0, 0)),
        ],
        out_specs=pl.BlockSpec((1, 1, r, d), lambda l, j: (l, j, 0, 0)),
        out_shape=jax.ShapeDtypeStruct((depth, 3, r, d), F32),
        compiler_params=_cparams(("arbitrary", "arbitrary")),
        name="modulation",
    )(c_all, w_mod, b_mod.reshape(depth, 3, 1, d))


def _inproj_kernel(x_ref, scale_ref, shift_ref, g_ref, w_ref, cos_ref, sin_up_ref, sin_dn_ref,
                   sbq_ref, sbk_ref, sbv_ref, sbg_ref, daq_ref, dak_ref, dav_ref, dag_ref,
                   sbk16_ref, sbv16_ref, dak16_ref, dav16_ref):
    x = x_ref[...]
    xn = x * lax.rsqrt(jnp.mean(x * x, axis=-1, keepdims=True) + EPS)
    h = (xn * g_ref[...]) * (1.0 + scale_ref[...]) + shift_ref[...]
    h16 = h.astype(BF16)

    def seg(i):
        return jnp.dot(h16, w_ref[:, i * GROUP_WIDTH:(i + 1) * GROUP_WIDTH],
                       preferred_element_type=F32)

    def rope_slab(p):
        return (p * cos_ref[...] + pltpu.roll(p, LANES - HEAD_DIM // 2, 1) * sin_up_ref[...]
                + pltpu.roll(p, HEAD_DIM // 2, 1) * sin_dn_ref[...])

    sbq_ref[...] = (seg(0) * Q_SCALE).astype(BF16)
    k = seg(1)
    sbk_ref[...] = k
    sbk16_ref[...] = k.astype(BF16)
    v = seg(2)
    sbv_ref[...] = v
    sbv16_ref[...] = v.astype(BF16)
    g = seg(3)
    sbg_ref[...] = g / (1.0 + jnp.exp(-g))
    q = seg(4)
    for s in range(SLABS):
        sl = slice(s * LANES, (s + 1) * LANES)
        daq_ref[:, sl] = (rope_slab(q[:, sl]) * (Q_SCALE * LOG2E)).astype(BF16)
    k = seg(5)
    for s in range(SLABS):
        sl = slice(s * LANES, (s + 1) * LANES)
        kr = rope_slab(k[:, sl])
        dak_ref[:, sl] = kr
        dak16_ref[:, sl] = kr.astype(BF16)
    v = seg(6)
    dav_ref[...] = v
    dav16_ref[...] = v.astype(BF16)
    g = seg(7)
    dag_ref[...] = g / (1.0 + jnp.exp(-g))


def _inproj(x, scale, shift, norm_g, w16, cos_t, sin_up, sin_dn, *, tm, rows_per_group):
    rows, d = x.shape
    nblk = rows // tm
    if rows_per_group is None:
        mod_spec = pl.BlockSpec((tm, d), lambda i: (i, 0))
    else:
        per = rows_per_group // tm
        mod_spec = pl.BlockSpec((None, 1, d), lambda i: (i // per, 0, 0))
    tab_blocks = cos_t.shape[0] // tm
    tab_spec = pl.BlockSpec((tm, LANES), lambda i: (i % tab_blocks, 0))
    out_spec = pl.BlockSpec((tm, GROUP_WIDTH), lambda i: (i, 0))
    f32o = jax.ShapeDtypeStruct((rows, GROUP_WIDTH), F32)
    b16o = jax.ShapeDtypeStruct((rows, GROUP_WIDTH), BF16)
    return pl.pallas_call(
        _inproj_kernel,
        grid=(nblk,),
        in_specs=[
            pl.BlockSpec((tm, d), lambda i: (i, 0)),
            mod_spec, mod_spec,
            pl.BlockSpec((1, d), lambda i: (0, 0)),
            pl.BlockSpec(w16.shape, lambda i: (0, 0)),
            tab_spec, tab_spec, tab_spec,
        ],
        out_specs=[out_spec] * 12,
        out_shape=[b16o, f32o, f32o, f32o, b16o, f32o, f32o, f32o, b16o, b16o, b16o, b16o],
        compiler_params=_cparams(("arbitrary",)),
        name="inproj",
    )(x, scale, shift, norm_g, w16, cos_t, sin_up, sin_dn)


def _nt_dot(a, b):
    return lax.dot_general(a, b, (((1,), (1,)), ((), ())), preferred_element_type=F32)


def _suffix_matrix(tk):
    j = lax.broadcasted_iota(jnp.int32, (tk, tk), 0)
    s = lax.broadcasted_iota(jnp.int32, (tk, tk), 1)
    return jnp.where(j > s, 1.0, 0.0).astype(BF16)


def _sb_weights(z, run, valid, upper):
    e = jnp.log(1.0 + jnp.exp(-jnp.abs(z)))
    sp = jnp.maximum(z, 0.0) + e
    ls = jnp.minimum(z, 0.0) - e
    if valid is not None:
        sp = jnp.where(valid, sp, 0.0)
    sp_hi = sp.astype(BF16)
    sp_lo = (sp - sp_hi.astype(F32)).astype(BF16)
    after = (jnp.dot(sp_hi, upper, preferred_element_type=F32)
             + jnp.dot(sp_lo, upper, preferred_element_type=F32))
    a = jnp.exp(ls - (after + run))
    if valid is not None:
        a = jnp.where(valid, a, 0.0)
    return a.astype(BF16), run + jnp.sum(sp, axis=-1, keepdims=True)


def _fold_lanes(p):
    out = p[:, :LANES]
    for c in range(1, p.shape[1] // LANES):
        out = out + p[:, c * LANES:(c + 1) * LANES]
    return out


def _da_weights(tiles, m, l, valid):
    if valid is not None:
        tiles = [jnp.where(valid, s, -jnp.inf) for s in tiles]
    top = tiles[0]
    for s in tiles[1:]:
        top = jnp.maximum(top, s)
    m_new = jnp.maximum(m, jnp.max(top, axis=-1, keepdims=True))
    alpha = jnp.exp2(m - m_new)
    l_new = alpha * l
    ps = []
    for s in tiles:
        p = jnp.exp2(s - m_new)
        l_new = l_new + _fold_lanes(p)
        ps.append(p.astype(BF16))
    return m_new, alpha, ps, l_new


def _lambda(lam_ref, lam_init):
    lam = lam_ref[...]
    s1 = jnp.sum(lam[0:1] * lam[1:2], axis=-1, keepdims=True)
    s2 = jnp.sum(lam[2:3] * lam[3:4], axis=-1, keepdims=True)
    return jnp.exp(s1) - jnp.exp(s2) + lam_init


def _da_finish(acc1, l1, acc2, l2, lam, subln_g, lam_init, gate):
    o = (acc1 / jnp.sum(l1, axis=-1, keepdims=True)
         - lam * (acc2 / jnp.sum(l2, axis=-1, keepdims=True)))
    on = o * lax.rsqrt(jnp.mean(o * o, axis=-1, keepdims=True) + EPS)
    return ((on * subln_g) * (1.0 - lam_init) * gate).astype(BF16)


def _stack_queries(q):
    lane = lax.broadcasted_iota(jnp.int32, q.shape, 1)
    zero = jnp.zeros_like(q)
    return jnp.concatenate([jnp.where(lane < HEAD_DIM, q, zero),
                            jnp.where(lane < HEAD_DIM, zero, q)], axis=0)


def _row_col(m, tk, tq):
    row = lax.broadcasted_iota(jnp.int32, (m, tk), 0)
    col = lax.broadcasted_iota(jnp.int32, (m, tk), 1)
    return jnp.where(row >= tq, row - tq, row), col


def _sb_prompt_kernel(q_ref, k_ref, v_ref, g_ref, o_ref, *, blk):
    i = pl.program_id(2)
    q2 = _stack_queries(q_ref[0])
    upper = _suffix_matrix(blk)
    t, s = _row_col(2 * blk, blk, blk)

    def kv(j):
        start = pl.multiple_of(j * blk, blk)
        return k_ref[0, pl.ds(start, blk), :], v_ref[0, pl.ds(start, blk), :]

    def block(j, run, valid):
        k, v = kv(j)
        a, run = _sb_weights(_nt_dot(q2, k), run, valid, upper)
        return jnp.dot(a, v, preferred_element_type=F32), run

    acc, run = block(i, jnp.zeros((2 * blk, 1), F32), s < t)

    def cond(c):
        j, _, run = c
        return jnp.logical_and(j >= 0, jnp.min(run) < EXP_ZERO_ABOVE)

    def body(c):
        j, acc, run = c
        pv, run = block(j, run, None)
        return j - 1, acc + pv, run

    _, acc, _ = lax.while_loop(cond, body, (i - 1, acc, run))
    lane = lax.broadcasted_iota(jnp.int32, (blk, LANES), 1)
    o = jnp.where(lane < HEAD_DIM, acc[:blk], acc[blk:])
    o_ref[0] = (o * g_ref[0]).astype(BF16)


def _da_prompt_kernel(q_ref, k_ref, v_ref, g_ref, lam_ref, subln_ref, o_ref, *, blk, lam_init):
    i = pl.program_id(2)
    q2 = _stack_queries(q_ref[0])
    m2 = 2 * blk

    def block(j, c, valid):
        m, l, acc = c
        start = pl.multiple_of(j * blk, blk)
        k, v = k_ref[0, pl.ds(start, blk), :], v_ref[0, pl.ds(start, blk), :]
        m, alpha, (p,), l = _da_weights([_nt_dot(q2, k)], m, l, valid)
        return m, l, alpha * acc + jnp.dot(p, v, preferred_element_type=F32)

    init = (jnp.full((m2, 1), -jnp.inf, F32), jnp.zeros((m2, LANES), F32), jnp.zeros((m2, LANES), F32))
    c = lax.fori_loop(0, i, lambda j, c: block(j, c, None), init)
    t, s = _row_col(m2, blk, blk)
    _, l, acc = block(i, c, s <= t)
    o_ref[0] = _da_finish(acc[:blk], l[:blk], acc[blk:], l[blk:], _lambda(lam_ref, lam_init),
                          subln_ref[...], lam_init, g_ref[0])


def _prompt_attention(kernel, name, q, k, v, g, extra, *, blk):
    b, t, _ = q.shape
    qspec = pl.BlockSpec((1, blk, LANES), lambda bi, s, i: (bi, i, s))
    kvspec = pl.BlockSpec((1, t, LANES), lambda bi, s, i: (bi, 0, s))
    extra_specs = [pl.BlockSpec(e.shape, lambda bi, s, i: (0, 0)) for e in extra]
    return pl.pallas_call(
        kernel,
        grid=(b, SLABS, t // blk),
        in_specs=[qspec, kvspec, kvspec, qspec] + extra_specs,
        out_specs=qspec,
        out_shape=jax.ShapeDtypeStruct((b, t, GROUP_WIDTH), BF16),
        compiler_params=_cparams(("arbitrary", "arbitrary", "arbitrary")),
        name=name,
    )(q, k, v, g, *extra)


def _sample_kernel(pt_ref, sbq_ref, sbkn_ref, sbvn_ref, sbg_ref, daq_ref, dakn_ref, davn_ref, dag_ref,
                   lam_ref, subln_ref, *rest, page, tq, n_step_pages, lam_init):
    del pt_ref
    pages = rest[:4 * n_step_pages]
    osb_ref, oda_ref = rest[4 * n_step_pages:4 * n_step_pages + 2]
    qbd_s, q2_s, sb_acc, sb_run, da_m, da_l, da_acc = rest[4 * n_step_pages + 2:]
    step = pl.program_id(1)
    last = pl.num_programs(1) - 1
    n_sb = SB_HEADS * tq
    n_da = 2 * DA_HEADS * tq
    upper = _suffix_matrix(page)

    def iota(shape, axis):
        return lax.broadcasted_iota(jnp.int32, shape, axis)

    da_head_of_row = iota((n_da, DA_HEADS * page), 0) // (2 * tq)
    da_page_valid = (iota((n_da, DA_HEADS * page), 1) % DA_HEADS) == da_head_of_row

    def pad_rows(x):
        return jnp.concatenate([x, jnp.zeros((page - x.shape[0], x.shape[1]), x.dtype)], axis=0)

    @pl.when(step == 0)
    def _():
        q = sbq_ref[0].astype(F32)
        q_rows = jnp.concatenate([q] * SB_HEADS, axis=0)
        own = (iota((n_sb, GROUP_WIDTH), 1) // HEAD_DIM) == (iota((n_sb, GROUP_WIDTH), 0) // tq)
        qbd = jnp.where(own, q_rows, 0.0).astype(BF16)
        qbd_s[...] = qbd
        q = daq_ref[0].astype(F32)
        lane = iota((tq, LANES), 1)
        pieces = []
        for h in range(DA_HEADS):
            slab = q[:, h * LANES:(h + 1) * LANES]
            pieces += [jnp.where(lane < HEAD_DIM, slab, 0.0), jnp.where(lane < HEAD_DIM, 0.0, slab)]
        q2 = jnp.concatenate(pieces, axis=0).astype(BF16)
        q2_s[...] = q2

        t_sb = iota((n_sb, page), 0) % tq
        a, run = _sb_weights(_nt_dot(qbd, pad_rows(sbkn_ref[0])), jnp.zeros((n_sb, 1), F32),
                             iota((n_sb, page), 1) < t_sb, upper)
        sb_acc[...] = jnp.dot(a, pad_rows(sbvn_ref[0]), preferred_element_type=F32)
        sb_run[...] = run

        c = iota((n_da, page), 1)
        r = iota((n_da, page), 0)
        valid = jnp.logical_and((c % DA_HEADS) == r // (2 * tq), c // DA_HEADS <= r % tq)
        m, _, (p,), l = _da_weights([_nt_dot(q2, pad_rows(dakn_ref[0]))], jnp.full((n_da, 1), -jnp.inf, F32),
                                    jnp.zeros((n_da, LANES), F32), valid)
        da_m[...] = m
        da_l[...] = l
        da_acc[...] = jnp.dot(p, pad_rows(davn_ref[0]), preferred_element_type=F32)

    for g0 in range(0, n_step_pages, SB_PAGE_GROUP):
        group = range(g0, min(g0 + SB_PAGE_GROUP, n_step_pages))

        @pl.when(jnp.min(sb_run[...]) < EXP_ZERO_ABOVE)
        def _():
            run = sb_run[...]
            acc = sb_acc[...]
            for i in group:
                kt = pages[4 * i][0, 0].astype(BF16)
                vt = pages[4 * i + 1][0, 0].astype(BF16)
                a, run = _sb_weights(jnp.dot(qbd_s[...], kt, preferred_element_type=F32), run, None, upper)
                acc = acc + _nt_dot(a, vt)
            sb_acc[...] = acc
            sb_run[...] = run

    tiles = [_nt_dot(q2_s[...], pages[4 * i + 2][0, 0].astype(BF16))
             for i in range(n_step_pages)]
    m, alpha, ps, l = _da_weights(tiles, da_m[...], da_l[...], da_page_valid)
    acc = alpha * da_acc[...]
    for i, p in enumerate(ps):
        acc = acc + jnp.dot(p, pages[4 * i + 3][0, 0].astype(BF16), preferred_element_type=F32)
    da_m[...] = m
    da_l[...] = l
    da_acc[...] = acc

    @pl.when(step == last)
    def _():
        acc = sb_acc[...]
        col_head = iota((tq, GROUP_WIDTH), 1) // HEAD_DIM
        o = jnp.zeros((tq, GROUP_WIDTH), F32)
        for h in range(SB_HEADS):
            o = o + jnp.where(col_head == h, acc[h * tq:(h + 1) * tq], 0.0)
        osb_ref[0] = (o * sbg_ref[0]).astype(BF16)

        lam = _lambda(lam_ref, lam_init)
        acc, l = da_acc[...], da_l[...]
        for h in range(DA_HEADS):
            r1, r2 = 2 * h * tq, (2 * h + 1) * tq
            oda_ref[0, :, h * LANES:(h + 1) * LANES] = _da_finish(
                acc[r1:r1 + tq], l[r1:r1 + tq], acc[r2:r2 + tq], l[r2:r2 + tq], lam, subln_ref[...],
                lam_init, dag_ref[0, :, h * LANES:(h + 1) * LANES])


def _sample_attention(page_table, new, caches, layer, lam_vecs, subln_g, *, lam_init):
    sbq = new[0]
    db, tq, _ = sbq.shape
    n_pages = page_table.shape[1]
    page = caches[0].shape[3]
    n_step_pages = math.gcd(PAGES_PER_STEP, n_pages)

    def new_spec(a):
        return pl.BlockSpec((1,) + a.shape[1:], lambda b, st, pt: (b, 0, 0))

    def page_spec(i):
        def index(b, st, pt):
            return (layer, pt[b, n_pages - 1 - (st * n_step_pages + i)], 0, 0)
        return pl.BlockSpec((1, 1, GROUP_WIDTH, page), index)

    small = lambda e: pl.BlockSpec(e.shape, lambda b, st, pt: (0, 0))
    page_specs, page_args = [], []
    for i in range(n_step_pages):
        page_specs += [page_spec(i)] * 4
        page_args += list(caches)
    n_sb, n_da = SB_HEADS * tq, 2 * DA_HEADS * tq
    out = jax.ShapeDtypeStruct((db, tq, GROUP_WIDTH), BF16)
    out_spec = pl.BlockSpec((1, tq, GROUP_WIDTH), lambda b, st, pt: (b, 0, 0))
    grid_spec = pltpu.PrefetchScalarGridSpec(
        num_scalar_prefetch=1,
        grid=(db, n_pages // n_step_pages),
        in_specs=[new_spec(a) for a in new] + [small(lam_vecs), small(subln_g)] + page_specs,
        out_specs=[out_spec, out_spec],
        scratch_shapes=[
            pltpu.VMEM((n_sb, GROUP_WIDTH), BF16), pltpu.VMEM((n_da, LANES), BF16),
            pltpu.VMEM((n_sb, GROUP_WIDTH), F32), pltpu.VMEM((n_sb, 1), F32),
            pltpu.VMEM((n_da, 1), F32), pltpu.VMEM((n_da, LANES), F32), pltpu.VMEM((n_da, LANES), F32),
        ],
    )
    return pl.pallas_call(
        functools.partial(_sample_kernel, page=page, tq=tq, n_step_pages=n_step_pages, lam_init=lam_init),
        grid_spec=grid_spec,
        out_shape=[out, out],
        compiler_params=_cparams(("arbitrary", "arbitrary")),
        name="sample_attention",
    )(page_table, *new, lam_vecs, subln_g, *page_args)


def _outproj_kernel(x_ref, msb_ref, mda_ref, w_ref, gate_ref, fg_ref, o_ref, *, final):
    half = msb_ref.shape[1]
    y = (jnp.dot(msb_ref[...], w_ref[:half], preferred_element_type=F32)
         + jnp.dot(mda_ref[...], w_ref[half:], preferred_element_type=F32))
    y = x_ref[...] + gate_ref[...] * y
    if final:
        y = y * lax.rsqrt(jnp.mean(y * y, axis=-1, keepdims=True) + EPS) * fg_ref[...]
    o_ref[...] = y


def _outproj(x, m_sb, m_da, w16, gate, final_g, *, tm, rows_per_group, final):
    rows, d = x.shape
    if rows_per_group is None:
        gate_spec = pl.BlockSpec((tm, d), lambda i: (i, 0))
    else:
        per = rows_per_group // tm
        gate_spec = pl.BlockSpec((None, 1, d), lambda i: (i // per, 0, 0))
    row_spec = pl.BlockSpec((tm, d), lambda i: (i, 0))
    mix_spec = pl.BlockSpec((tm, GROUP_WIDTH), lambda i: (i, 0))
    return pl.pallas_call(
        functools.partial(_outproj_kernel, final=final),
        grid=(rows // tm,),
        in_specs=[row_spec, mix_spec, mix_spec, pl.BlockSpec(w16.shape, lambda i: (0, 0)),
                  gate_spec, pl.BlockSpec((1, d), lambda i: (0, 0))],
        out_specs=row_spec,
        out_shape=jax.ShapeDtypeStruct((rows, d), F32),
        compiler_params=_cparams(("arbitrary",)),
        name="outproj",
    )(x, m_sb, m_da, w16, gate, final_g)


def _rope_tables(pos):
    inv_freq = ROPE_THETA ** (-jnp.arange(0, HEAD_DIM, 2, dtype=F32) / HEAD_DIM)
    ang = pos.astype(F32)[:, None] * inv_freq[None, :]
    ang = jnp.concatenate([ang, ang, ang, ang], axis=-1)
    first_half = (jnp.arange(LANES) % HEAD_DIM) < HEAD_DIM // 2
    sin = jnp.sin(ang)
    return jnp.cos(ang), jnp.where(first_half, -sin, 0.0), jnp.where(first_half, 0.0, sin)


def _pick_tile(rows, target):
    tm = min(rows, target)
    while rows % tm:
        tm //= 2
    return tm


def kernel(x_prompt, x_sample, c_prompt, c_sample, cache_sb_k, cache_sb_v, cache_da_k, cache_da_v,
           page_table, w_mod, b_mod, norm_g, w_in, w_out, lambda_q1, lambda_k1, lambda_q2, lambda_k2,
           subln_g, final_norm_g):
    bsz, seq, d = x_prompt.shape
    db, dseq, _ = x_sample.shape
    depth = w_in.shape[0]
    n_pool, page = cache_sb_k.shape[1], cache_sb_k.shape[2]
    n_pages = page_table.shape[1]
    past_len = n_pages * page

    n_c = bsz + db
    c_all = jnp.concatenate([c_prompt, c_sample, jnp.zeros((-n_c % 8, d), F32)], axis=0)
    mod = _modulation(c_all, w_mod, b_mod)

    tabs_p = _rope_tables(jnp.arange(seq, dtype=jnp.int32))
    tabs_s = tuple(jnp.tile(t, (db, 1))
                   for t in _rope_tables(past_len + jnp.arange(dseq, dtype=jnp.int32)))

    caches = tuple(jnp.transpose(c, (0, 1, 3, 4, 2)).reshape(depth, n_pool, GROUP_WIDTH, page)
                   for c in (cache_sb_k, cache_sb_v))
    caches += tuple(c.reshape(depth, n_pool, page * DA_HEADS, 2 * HEAD_DIM)
                    for c in (cache_da_k, cache_da_v))
    w_in16 = w_in.astype(BF16)
    w_out16 = w_out.astype(BF16)
    lam_all = jnp.stack([lambda_q1, lambda_k1, lambda_q2, lambda_k2], axis=1)
    fg = final_norm_g.reshape(1, d)

    rows_p, rows_s = bsz * seq, db * dseq
    tm_p = _pick_tile(seq, 256)
    tm_s = _pick_tile(rows_s, 256)
    sb_blk = _pick_tile(seq, SB_BLOCK)
    da_blk = _pick_tile(seq, DA_BLOCK)

    hp = x_prompt.reshape(rows_p, d)
    hs = x_sample.reshape(rows_s, d)
    new_p, new_s = [], []
    for l in range(depth):
        lam_init = 0.8 - 0.6 * math.exp(-0.3 * l)
        g_l = norm_g[l].reshape(1, d)
        subln = subln_g[l].reshape(1, LANES)
        final = l == depth - 1

        def per_row(v):
            return jnp.broadcast_to(v[:, None, :], (db, dseq, d)).reshape(rows_s, d)

        shift_p, scale_p, gate_p = (mod[l, j, :bsz].reshape(bsz, 1, d) for j in range(3))
        shift_s, scale_s, gate_s = (per_row(mod[l, j, bsz:n_c]) for j in range(3))

        (sbq, sbk, sbv, sbg, daq, dak, dav, dag, sbk16, sbv16, dak16, dav16) = _inproj(
            hp, scale_p, shift_p, g_l, w_in16[l], *tabs_p, tm=tm_p, rows_per_group=seq)
        new_p.append((sbk, sbv, dak, dav))
        r3 = lambda a: a.reshape(bsz, seq, GROUP_WIDTH)
        m_sb = _prompt_attention(functools.partial(_sb_prompt_kernel, blk=sb_blk), "sb_prompt",
                                 r3(sbq), r3(sbk16), r3(sbv16), r3(sbg), (), blk=sb_blk)
        m_da = _prompt_attention(functools.partial(_da_prompt_kernel, blk=da_blk, lam_init=lam_init),
                                 "da_prompt", r3(daq), r3(dak16), r3(dav16), r3(dag),
                                 (lam_all[l], subln), blk=da_blk)
        hp = _outproj(hp, m_sb.reshape(rows_p, GROUP_WIDTH), m_da.reshape(rows_p, GROUP_WIDTH),
                      w_out16[l], gate_p, fg, tm=tm_p, rows_per_group=seq, final=final)

        (sbq, sbk, sbv, sbg, daq, dak, dav, dag, sbk16, sbv16, dak16, dav16) = _inproj(
            hs, scale_s, shift_s, g_l, w_in16[l], *tabs_s, tm=tm_s, rows_per_group=None)
        new_s.append((sbk, sbv, dak, dav))
        s3 = lambda a: a.reshape(db, dseq, GROUP_WIDTH)
        da_rows = lambda a: a.reshape(db, dseq * DA_HEADS, 2 * HEAD_DIM)
        m_sb, m_da = _sample_attention(
            page_table, (s3(sbq), s3(sbk16), s3(sbv16), s3(sbg), s3(daq), da_rows(dak16), da_rows(dav16),
                         s3(dag)),
            caches, l, lam_all[l], subln, lam_init=lam_init)
        hs = _outproj(hs, m_sb.reshape(rows_s, GROUP_WIDTH), m_da.reshape(rows_s, GROUP_WIDTH),
                      w_out16[l], gate_s, fg, tm=tm_s, rows_per_group=None, final=final)

    def stack(rows, idx, lead, heads):
        return jnp.stack([r[idx] for r in rows], axis=0).reshape((depth,) + lead + heads)

    sb_heads = (SB_HEADS, HEAD_DIM)
    da_heads = (DA_HEADS, 2 * HEAD_DIM)
    outs_p = [stack(new_p, i, (bsz, seq), sb_heads if i < 2 else da_heads) for i in range(4)]
    outs_s = [stack(new_s, i, (db, dseq), sb_heads if i < 2 else da_heads) for i in range(4)]
    return (hp.reshape(bsz, seq, d), hs.reshape(db, dseq, d), *outs_p, *outs_s)
```

```python
import functools
import math

import jax
import jax.numpy as jnp
from jax import lax
from jax.experimental import pallas as pl
from jax.experimental.pallas import tpu as pltpu

F32 = jnp.float32
BF16 = jnp.bfloat16

HEAD_DIM = 64
LANES = 128
SLABS = 4
GROUP_WIDTH = SLABS * LANES
SB_HEADS = GROUP_WIDTH // HEAD_DIM
DA_HEADS = GROUP_WIDTH // (2 * HEAD_DIM)
ROPE_THETA = 10000.0
EPS = 1e-6
Q_SCALE = HEAD_DIM ** -0.5
LOG2E = math.log2(math.e)
EXP_ZERO_ABOVE = 104.0
VMEM_LIMIT = 48 * 1024 * 1024
SB_BLOCK = 256
DA_BLOCK = 512
PAGES_PER_STEP = 8
SB_PAGE_GROUP = 2


def _cparams(sem):
    return pltpu.CompilerParams(dimension_semantics=sem, vmem_limit_bytes=VMEM_LIMIT)


def _mod_kernel(c_ref, w_ref, b_ref, o_ref):
    c = c_ref[...]
    sc = c / (1.0 + jnp.exp(-c))
    o_ref[0, 0] = jnp.dot(sc, w_ref[0], precision=lax.Precision.HIGHEST,
                          preferred_element_type=F32) + b_ref[0, 0]


def _modulation(c_all, w_mod, b_mod):
    depth, d, _ = w_mod.shape
    r = c_all.shape[0]
    return pl.pallas_call(
        _mod_kernel,
        grid=(depth, 3),
        in_specs=[
            pl.BlockSpec((r, d), lambda l, j: (0, 0)),
            pl.BlockSpec((1, d, d), lambda l, j: (l, 0, j)),
            pl.BlockSpec((1, 1, 1, d), lambda l, j: (l, j, 0, 0)),
        ],
        out_specs=pl.BlockSpec((1, 1, r, d), lambda l, j: (l, j, 0, 0)),
        out_shape=jax.ShapeDtypeStruct((depth, 3, r, d), F32),
        compiler_params=_cparams(("arbitrary", "arbitrary")),
        name="modulation",
    )(c_all, w_mod, b_mod.reshape(depth, 3, 1, d))


def _inproj_kernel(x_ref, scale_ref, shift_ref, g_ref, w_ref, cos_ref, sin_up_ref, sin_dn_ref,
                   sbq_ref, sbk_ref, sbv_ref, sbg_ref, daq_ref, dak_ref, dav_ref, dag_ref,
                   sbk16_ref, sbv16_ref, dak16_ref, dav16_ref):
    x = x_ref[...]
    xn = x * lax.rsqrt(jnp.mean(x * x, axis=-1, keepdims=True) + EPS)
    h = (xn * g_ref[...]) * (1.0 + scale_ref[...]) + shift_ref[...]
    h16 = h.astype(BF16)

    def seg(i):
        return jnp.dot(h16, w_ref[:, i * GROUP_WIDTH:(i + 1) * GROUP_WIDTH],
                       preferred_element_type=F32)

    def rope_slab(p):
        return (p * cos_ref[...] + pltpu.roll(p, LANES - HEAD_DIM // 2, 1) * sin_up_ref[...]
                + pltpu.roll(p, HEAD_DIM // 2, 1) * sin_dn_ref[...])

    sbq_ref[...] = (seg(0) * Q_SCALE).astype(BF16)
    k = seg(1)
    sbk_ref[...] = k
    sbk16_ref[...] = k.astype(BF16)
    v = seg(2)
    sbv_ref[...] = v
    sbv16_ref[...] = v.astype(BF16)
    g = seg(3)
    sbg_ref[...] = g / (1.0 + jnp.exp(-g))
    q = seg(4)
    for s in range(SLABS):
        sl = slice(s * LANES, (s + 1) * LANES)
        daq_ref[:, sl] = (rope_slab(q[:, sl]) * (Q_SCALE * LOG2E)).astype(BF16)
    k = seg(5)
    for s in range(SLABS):
        sl = slice(s * LANES, (s + 1) * LANES)
        kr = rope_slab(k[:, sl])
        dak_ref[:, sl] = kr
        dak16_ref[:, sl] = kr.astype(BF16)
    v = seg(6)
    dav_ref[...] = v
    dav16_ref[...] = v.astype(BF16)
    g = seg(7)
    dag_ref[...] = g / (1.0 + jnp.exp(-g))


def _inproj(x, scale, shift, norm_g, w16, cos_t, sin_up, sin_dn, *, tm, rows_per_group):
    rows, d = x.shape
    nblk = rows // tm
    if rows_per_group is None:
        mod_spec = pl.BlockSpec((tm, d), lambda i: (i, 0))
    else:
        per = rows_per_group // tm
        mod_spec = pl.BlockSpec((None, 1, d), lambda i: (i // per, 0, 0))
    tab_blocks = cos_t.shape[0] // tm
    tab_spec = pl.BlockSpec((tm, LANES), lambda i: (i % tab_blocks, 0))
    out_spec = pl.BlockSpec((tm, GROUP_WIDTH), lambda i: (i, 0))
    f32o = jax.ShapeDtypeStruct((rows, GROUP_WIDTH), F32)
    b16o = jax.ShapeDtypeStruct((rows, GROUP_WIDTH), BF16)
    return pl.pallas_call(
        _inproj_kernel,
        grid=(nblk,),
        in_specs=[
            pl.BlockSpec((tm, d), lambda i: (i, 0)),
            mod_spec, mod_spec,
            pl.BlockSpec((1, d), lambda i: (0, 0)),
            pl.BlockSpec(w16.shape, lambda i: (0, 0)),
            tab_spec, tab_spec, tab_spec,
        ],
        out_specs=[out_spec] * 12,
        out_shape=[b16o, f32o, f32o, f32o, b16o, f32o, f32o, f32o, b16o, b16o, b16o, b16o],
        compiler_params=_cparams(("arbitrary",)),
        name="inproj",
    )(x, scale, shift, norm_g, w16, cos_t, sin_up, sin_dn)


def _nt_dot(a, b):
    return lax.dot_general(a, b, (((1,), (1,)), ((), ())), preferred_element_type=F32)


def _suffix_matrix(tk):
    j = lax.broadcasted_iota(jnp.int32, (tk, tk), 0)
    s = lax.broadcasted_iota(jnp.int32, (tk, tk), 1)
    return jnp.where(j > s, 1.0, 0.0).astype(BF16)


def _sb_weights(z, run, valid, upper):
    e = jnp.log(1.0 + jnp.exp(-jnp.abs(z)))
    sp = jnp.maximum(z, 0.0) + e
    ls = jnp.minimum(z, 0.0) - e
    if valid is not None:
        sp = jnp.where(valid, sp, 0.0)
    sp_hi = sp.astype(BF16)
    sp_lo = (sp - sp_hi.astype(F32)).astype(BF16)
    after = (jnp.dot(sp_hi, upper, preferred_element_type=F32)
             + jnp.dot(sp_lo, upper, preferred_element_type=F32))
    a = jnp.exp(ls - (after + run))
    if valid is not None:
        a = jnp.where(valid, a, 0.0)
    return a.astype(BF16), run + jnp.sum(sp, axis=-1, keepdims=True)


def _lane_chunks(x):
    return [x[:, c * LANES:(c + 1) * LANES] for c in range(x.shape[1] // LANES)]


def _da_weights(tiles, m, l, valid):
    if valid is not None:
        tiles = [jnp.where(valid, s, -jnp.inf) for s in tiles]
    chunks = [_lane_chunks(s) for s in tiles]
    top = chunks[0][0]
    for c in [c for cs in chunks for c in cs][1:]:
        top = jnp.maximum(top, c)
    m_new = jnp.maximum(m, jnp.broadcast_to(jnp.max(top, axis=-1, keepdims=True), m.shape))
    alpha = jnp.exp2(m - m_new)
    l_new = alpha * l
    ps = []
    for cs in chunks:
        pcs = [jnp.exp2(c - m_new) for c in cs]
        for pc in pcs:
            l_new = l_new + pc
        ps.append(jnp.concatenate([pc.astype(BF16) for pc in pcs], axis=1))
    return m_new, alpha, ps, l_new


def _lambda(lam_ref, lam_init):
    lam = lam_ref[...]
    s1 = jnp.sum(lam[0:1] * lam[1:2], axis=-1, keepdims=True)
    s2 = jnp.sum(lam[2:3] * lam[3:4], axis=-1, keepdims=True)
    return jnp.exp(s1) - jnp.exp(s2) + lam_init


def _da_finish(acc1, l1, acc2, l2, lam, subln_g, lam_init, gate):
    o = (acc1 / jnp.sum(l1, axis=-1, keepdims=True)
         - lam * (acc2 / jnp.sum(l2, axis=-1, keepdims=True)))
    on = o * lax.rsqrt(jnp.mean(o * o, axis=-1, keepdims=True) + EPS)
    return ((on * subln_g) * (1.0 - lam_init) * gate).astype(BF16)


def _stack_queries(q):
    lane = lax.broadcasted_iota(jnp.int32, q.shape, 1)
    zero = jnp.zeros_like(q)
    return jnp.concatenate([jnp.where(lane < HEAD_DIM, q, zero),
                            jnp.where(lane < HEAD_DIM, zero, q)], axis=0)


def _row_col(m, tk, tq):
    row = lax.broadcasted_iota(jnp.int32, (m, tk), 0)
    col = lax.broadcasted_iota(jnp.int32, (m, tk), 1)
    return jnp.where(row >= tq, row - tq, row), col


def _sb_prompt_kernel(q_ref, k_ref, v_ref, g_ref, o_ref, *, blk):
    i = pl.program_id(2)
    q2 = _stack_queries(q_ref[0])
    upper = _suffix_matrix(blk)
    t, s = _row_col(2 * blk, blk, blk)

    def kv(j):
        start = pl.multiple_of(j * blk, blk)
        return k_ref[0, pl.ds(start, blk), :], v_ref[0, pl.ds(start, blk), :]

    def block(j, run, valid):
        k, v = kv(j)
        a, run = _sb_weights(_nt_dot(q2, k), run, valid, upper)
        return jnp.dot(a, v, preferred_element_type=F32), run

    acc, run = block(i, jnp.zeros((2 * blk, 1), F32), s < t)

    def cond(c):
        j, _, run = c
        return jnp.logical_and(j >= 0, jnp.min(run) < EXP_ZERO_ABOVE)

    def body(c):
        j, acc, run = c
        pv, run = block(j, run, None)
        return j - 1, acc + pv, run

    _, acc, _ = lax.while_loop(cond, body, (i - 1, acc, run))
    lane = lax.broadcasted_iota(jnp.int32, (blk, LANES), 1)
    o = jnp.where(lane < HEAD_DIM, acc[:blk], acc[blk:])
    o_ref[0] = (o * g_ref[0]).astype(BF16)


def _da_prompt_kernel(q_ref, k_ref, v_ref, g_ref, lam_ref, subln_ref, o_ref,
                      s_even, s_odd, m_s, l_s, acc_s, *, blk, lam_init):
    i = pl.program_id(2)
    q2 = _stack_queries(q_ref[0])
    m2 = 2 * blk

    def scores_into(s_ref, j):
        start = pl.multiple_of(j * blk, blk)
        s_ref[...] = _nt_dot(q2, k_ref[0, pl.ds(start, blk), :])

    def absorb(s_ref, j, valid):
        start = pl.multiple_of(j * blk, blk)
        m, alpha, (p,), l = _da_weights([s_ref[...]], m_s[...], l_s[...], valid)
        m_s[...] = m
        l_s[...] = l
        acc_s[...] = alpha * acc_s[...] + jnp.dot(p, v_ref[0, pl.ds(start, blk), :],
                                                  preferred_element_type=F32)

    m_s[...] = jnp.full((m2, LANES), -jnp.inf, F32)
    l_s[...] = jnp.zeros((m2, LANES), F32)
    acc_s[...] = jnp.zeros((m2, LANES), F32)
    scores_into(s_even, 0)

    def pair(jj, carry):
        j = 2 * jj
        scores_into(s_odd, j + 1)
        absorb(s_even, j, None)
        scores_into(s_even, j + 2)
        absorb(s_odd, j + 1, None)
        return carry

    lax.fori_loop(0, i // 2, pair, 0)
    t, s = _row_col(m2, blk, blk)
    diag = s <= t

    @pl.when(i % 2 == 0)
    def _():
        absorb(s_even, i, diag)

    @pl.when(i % 2 == 1)
    def _():
        scores_into(s_odd, i)
        absorb(s_even, i - 1, None)
        absorb(s_odd, i, diag)

    acc, l = acc_s[...], l_s[...]
    o_ref[0] = _da_finish(acc[:blk], l[:blk], acc[blk:], l[blk:], _lambda(lam_ref, lam_init),
                          subln_ref[...], lam_init, g_ref[0])


def _prompt_attention(kernel, name, q, k, v, g, extra, scratch, *, blk):
    b, t, _ = q.shape
    qspec = pl.BlockSpec((1, blk, LANES), lambda bi, s, i: (bi, i, s))
    kvspec = pl.BlockSpec((1, t, LANES), lambda bi, s, i: (bi, 0, s))
    extra_specs = [pl.BlockSpec(e.shape, lambda bi, s, i: (0, 0)) for e in extra]
    return pl.pallas_call(
        kernel,
        grid=(b, SLABS, t // blk),
        in_specs=[qspec, kvspec, kvspec, qspec] + extra_specs,
        out_specs=qspec,
        out_shape=jax.ShapeDtypeStruct((b, t, GROUP_WIDTH), BF16),
        scratch_shapes=scratch,
        compiler_params=_cparams(("arbitrary", "arbitrary", "arbitrary")),
        name=name,
    )(q, k, v, g, *extra)


def _sample_kernel(pt_ref, sbq_ref, sbkn_ref, sbvn_ref, sbg_ref, daq_ref, dakn_ref, davn_ref, dag_ref,
                   lam_ref, subln_ref, *rest, page, tq, n_step_pages, lam_init):
    del pt_ref
    pages = rest[:4 * n_step_pages]
    osb_ref, oda_ref = rest[4 * n_step_pages:4 * n_step_pages + 2]
    qbd_s, q2_s, sb_acc, sb_run, da_m, da_l, da_acc = rest[4 * n_step_pages + 2:]
    step = pl.program_id(1)
    last = pl.num_programs(1) - 1
    n_sb = SB_HEADS * tq
    n_da = 2 * DA_HEADS * tq
    upper = _suffix_matrix(page)

    def iota(shape, axis):
        return lax.broadcasted_iota(jnp.int32, shape, axis)

    da_head_of_row = iota((n_da, DA_HEADS * page), 0) // (2 * tq)
    da_page_valid = (iota((n_da, DA_HEADS * page), 1) % DA_HEADS) == da_head_of_row

    def pad_rows(x):
        return jnp.concatenate([x, jnp.zeros((page - x.shape[0], x.shape[1]), x.dtype)], axis=0)

    @pl.when(step == 0)
    def _():
        q = sbq_ref[0].astype(F32)
        q_rows = jnp.concatenate([q] * SB_HEADS, axis=0)
        own = (iota((n_sb, GROUP_WIDTH), 1) // HEAD_DIM) == (iota((n_sb, GROUP_WIDTH), 0) // tq)
        qbd = jnp.where(own, q_rows, 0.0).astype(BF16)
        qbd_s[...] = qbd
        q = daq_ref[0].astype(F32)
        lane = iota((tq, LANES), 1)
        pieces = []
        for h in range(DA_HEADS):
            slab = q[:, h * LANES:(h + 1) * LANES]
            pieces += [jnp.where(lane < HEAD_DIM, slab, 0.0), jnp.where(lane < HEAD_DIM, 0.0, slab)]
        q2 = jnp.concatenate(pieces, axis=0).astype(BF16)
        q2_s[...] = q2

        t_sb = iota((n_sb, page), 0) % tq
        a, run = _sb_weights(_nt_dot(qbd, pad_rows(sbkn_ref[0])), jnp.zeros((n_sb, 1), F32),
                             iota((n_sb, page), 1) < t_sb, upper)
        sb_acc[...] = jnp.dot(a, pad_rows(sbvn_ref[0]), preferred_element_type=F32)
        sb_run[...] = run

        c = iota((n_da, page), 1)
        r = iota((n_da, page), 0)
        valid = jnp.logical_and((c % DA_HEADS) == r // (2 * tq), c // DA_HEADS <= r % tq)
        m, _, (p,), l = _da_weights([_nt_dot(q2, pad_rows(dakn_ref[0]))], jnp.full((n_da, LANES), -jnp.inf, F32),
                                    jnp.zeros((n_da, LANES), F32), valid)
        da_m[...] = m
        da_l[...] = l
        da_acc[...] = jnp.dot(p, pad_rows(davn_ref[0]), preferred_element_type=F32)

    def sb_groups(g0):
        if g0 >= n_step_pages:
            return

        @pl.when(jnp.min(sb_run[...]) < EXP_ZERO_ABOVE)
        def _():
            run = sb_run[...]
            acc = sb_acc[...]
            for i in range(g0, min(g0 + SB_PAGE_GROUP, n_step_pages)):
                kt = pages[4 * i][0, 0].astype(BF16)
                vt = pages[4 * i + 1][0, 0].astype(BF16)
                a, run = _sb_weights(jnp.dot(qbd_s[...], kt, preferred_element_type=F32), run, None, upper)
                acc = acc + _nt_dot(a, vt)
            sb_acc[...] = acc
            sb_run[...] = run
            sb_groups(g0 + SB_PAGE_GROUP)

    sb_groups(0)

    tiles = [_nt_dot(q2_s[...], pages[4 * i + 2][0, 0].astype(BF16))
             for i in range(n_step_pages)]
    m, alpha, ps, l = _da_weights(tiles, da_m[...], da_l[...], da_page_valid)
    acc = alpha * da_acc[...]
    for i, p in enumerate(ps):
        acc = acc + jnp.dot(p, pages[4 * i + 3][0, 0].astype(BF16), preferred_element_type=F32)
    da_m[...] = m
    da_l[...] = l
    da_acc[...] = acc

    @pl.when(step == last)
    def _():
        acc = sb_acc[...]
        col_head = iota((tq, GROUP_WIDTH), 1) // HEAD_DIM
        o = jnp.zeros((tq, GROUP_WIDTH), F32)
        for h in range(SB_HEADS):
            o = o + jnp.where(col_head == h, acc[h * tq:(h + 1) * tq], 0.0)
        osb_ref[0] = (o * sbg_ref[0]).astype(BF16)

        lam = _lambda(lam_ref, lam_init)
        acc, l = da_acc[...], da_l[...]
        for h in range(DA_HEADS):
            r1, r2 = 2 * h * tq, (2 * h + 1) * tq
            oda_ref[0, :, h * LANES:(h + 1) * LANES] = _da_finish(
                acc[r1:r1 + tq], l[r1:r1 + tq], acc[r2:r2 + tq], l[r2:r2 + tq], lam, subln_ref[...],
                lam_init, dag_ref[0, :, h * LANES:(h + 1) * LANES])


def _sample_attention(page_table, new, caches, layer, lam_vecs, subln_g, *, lam_init):
    sbq = new[0]
    db, tq, _ = sbq.shape
    n_pages = page_table.shape[1]
    page = caches[0].shape[3]
    n_step_pages = math.gcd(PAGES_PER_STEP, n_pages)

    def new_spec(a):
        return pl.BlockSpec((1,) + a.shape[1:], lambda b, st, pt: (b, 0, 0))

    def page_spec(i):
        def index(b, st, pt):
            return (layer, pt[b, n_pages - 1 - (st * n_step_pages + i)], 0, 0)
        return pl.BlockSpec((1, 1, GROUP_WIDTH, page), index)

    small = lambda e: pl.BlockSpec(e.shape, lambda b, st, pt: (0, 0))
    page_specs, page_args = [], []
    for i in range(n_step_pages):
        page_specs += [page_spec(i)] * 4
        page_args += list(caches)
    n_sb, n_da = SB_HEADS * tq, 2 * DA_HEADS * tq
    out = jax.ShapeDtypeStruct((db, tq, GROUP_WIDTH), BF16)
    out_spec = pl.BlockSpec((1, tq, GROUP_WIDTH), lambda b, st, pt: (b, 0, 0))
    grid_spec = pltpu.PrefetchScalarGridSpec(
        num_scalar_prefetch=1,
        grid=(db, n_pages // n_step_pages),
        in_specs=[new_spec(a) for a in new] + [small(lam_vecs), small(subln_g)] + page_specs,
        out_specs=[out_spec, out_spec],
        scratch_shapes=[
            pltpu.VMEM((n_sb, GROUP_WIDTH), BF16), pltpu.VMEM((n_da, LANES), BF16),
            pltpu.VMEM((n_sb, GROUP_WIDTH), F32), pltpu.VMEM((n_sb, 1), F32),
            pltpu.VMEM((n_da, LANES), F32), pltpu.VMEM((n_da, LANES), F32), pltpu.VMEM((n_da, LANES), F32),
        ],
    )
    return pl.pallas_call(
        functools.partial(_sample_kernel, page=page, tq=tq, n_step_pages=n_step_pages, lam_init=lam_init),
        grid_spec=grid_spec,
        out_shape=[out, out],
        compiler_params=_cparams(("arbitrary", "arbitrary")),
        name="sample_attention",
    )(page_table, *new, lam_vecs, subln_g, *page_args)


def _outproj_kernel(x_ref, msb_ref, mda_ref, w_ref, gate_ref, fg_ref, o_ref, *, final):
    half = msb_ref.shape[1]
    y = (jnp.dot(msb_ref[...], w_ref[:half], preferred_element_type=F32)
         + jnp.dot(mda_ref[...], w_ref[half:], preferred_element_type=F32))
    y = x_ref[...] + gate_ref[...] * y
    if final:
        y = y * lax.rsqrt(jnp.mean(y * y, axis=-1, keepdims=True) + EPS) * fg_ref[...]
    o_ref[...] = y


def _outproj(x, m_sb, m_da, w16, gate, final_g, *, tm, rows_per_group, final):
    rows, d = x.shape
    if rows_per_group is None:
        gate_spec = pl.BlockSpec((tm, d), lambda i: (i, 0))
    else:
        per = rows_per_group // tm
        gate_spec = pl.BlockSpec((None, 1, d), lambda i: (i // per, 0, 0))
    row_spec = pl.BlockSpec((tm, d), lambda i: (i, 0))
    mix_spec = pl.BlockSpec((tm, GROUP_WIDTH), lambda i: (i, 0))
    return pl.pallas_call(
        functools.partial(_outproj_kernel, final=final),
        grid=(rows // tm,),
        in_specs=[row_spec, mix_spec, mix_spec, pl.BlockSpec(w16.shape, lambda i: (0, 0)),
                  gate_spec, pl.BlockSpec((1, d), lambda i: (0, 0))],
        out_specs=row_spec,
        out_shape=jax.ShapeDtypeStruct((rows, d), F32),
        compiler_params=_cparams(("arbitrary",)),
        name="outproj",
    )(x, m_sb, m_da, w16, gate, final_g)


def _rope_tables(pos):
    inv_freq = ROPE_THETA ** (-jnp.arange(0, HEAD_DIM, 2, dtype=F32) / HEAD_DIM)
    ang = pos.astype(F32)[:, None] * inv_freq[None, :]
    ang = jnp.concatenate([ang, ang, ang, ang], axis=-1)
    first_half = (jnp.arange(LANES) % HEAD_DIM) < HEAD_DIM // 2
    sin = jnp.sin(ang)
    return jnp.cos(ang), jnp.where(first_half, -sin, 0.0), jnp.where(first_half, 0.0, sin)


def _pick_tile(rows, target):
    tm = min(rows, target)
    while rows % tm:
        tm //= 2
    return tm


def kernel(x_prompt, x_sample, c_prompt, c_sample, cache_sb_k, cache_sb_v, cache_da_k, cache_da_v,
           page_table, w_mod, b_mod, norm_g, w_in, w_out, lambda_q1, lambda_k1, lambda_q2, lambda_k2,
           subln_g, final_norm_g):
    bsz, seq, d = x_prompt.shape
    db, dseq, _ = x_sample.shape
    depth = w_in.shape[0]
    n_pool, page = cache_sb_k.shape[1], cache_sb_k.shape[2]
    n_pages = page_table.shape[1]
    past_len = n_pages * page

    n_c = bsz + db
    c_all = jnp.concatenate([c_prompt, c_sample, jnp.zeros((-n_c % 8, d), F32)], axis=0)
    mod = _modulation(c_all, w_mod, b_mod)

    tabs_p = _rope_tables(jnp.arange(seq, dtype=jnp.int32))
    tabs_s = tuple(jnp.tile(t, (db, 1))
                   for t in _rope_tables(past_len + jnp.arange(dseq, dtype=jnp.int32)))

    caches = tuple(jnp.transpose(c, (0, 1, 3, 4, 2)).reshape(depth, n_pool, GROUP_WIDTH, page)
                   for c in (cache_sb_k, cache_sb_v))
    caches += tuple(c.reshape(depth, n_pool, page * DA_HEADS, 2 * HEAD_DIM)
                    for c in (cache_da_k, cache_da_v))
    w_in16 = w_in.astype(BF16)
    w_out16 = w_out.astype(BF16)
    lam_all = jnp.stack([lambda_q1, lambda_k1, lambda_q2, lambda_k2], axis=1)
    fg = final_norm_g.reshape(1, d)

    rows_p, rows_s = bsz * seq, db * dseq
    tm_p = _pick_tile(seq, 256)
    tm_s = _pick_tile(rows_s, 256)
    sb_blk = _pick_tile(seq, SB_BLOCK)
    da_blk = _pick_tile(seq, DA_BLOCK)

    hp = x_prompt.reshape(rows_p, d)
    hs = x_sample.reshape(rows_s, d)
    new_p, new_s = [], []
    for l in range(depth):
        lam_init = 0.8 - 0.6 * math.exp(-0.3 * l)
        g_l = norm_g[l].reshape(1, d)
        subln = subln_g[l].reshape(1, LANES)
        final = l == depth - 1

        def per_row(v):
            return jnp.broadcast_to(v[:, None, :], (db, dseq, d)).reshape(rows_s, d)

        shift_p, scale_p, gate_p = (mod[l, j, :bsz].reshape(bsz, 1, d) for j in range(3))
        shift_s, scale_s, gate_s = (per_row(mod[l, j, bsz:n_c]) for j in range(3))

        (sbq, sbk, sbv, sbg, daq, dak, dav, dag, sbk16, sbv16, dak16, dav16) = _inproj(
            hp, scale_p, shift_p, g_l, w_in16[l], *tabs_p, tm=tm_p, rows_per_group=seq)
        new_p.append((sbk, sbv, dak, dav))
        r3 = lambda a: a.reshape(bsz, seq, GROUP_WIDTH)
        m_sb = _prompt_attention(functools.partial(_sb_prompt_kernel, blk=sb_blk), "sb_prompt",
                                 r3(sbq), r3(sbk16), r3(sbv16), r3(sbg), (), [], blk=sb_blk)
        da_scratch = [pltpu.VMEM((2 * da_blk, da_blk), F32), pltpu.VMEM((2 * da_blk, da_blk), F32),
                      pltpu.VMEM((2 * da_blk, LANES), F32), pltpu.VMEM((2 * da_blk, LANES), F32),
                      pltpu.VMEM((2 * da_blk, LANES), F32)]
        m_da = _prompt_attention(functools.partial(_da_prompt_kernel, blk=da_blk, lam_init=lam_init),
                                 "da_prompt", r3(daq), r3(dak16), r3(dav16), r3(dag),
                                 (lam_all[l], subln), da_scratch, blk=da_blk)
        hp = _outproj(hp, m_sb.reshape(rows_p, GROUP_WIDTH), m_da.reshape(rows_p, GROUP_WIDTH),
                      w_out16[l], gate_p, fg, tm=tm_p, rows_per_group=seq, final=final)

        (sbq, sbk, sbv, sbg, daq, dak, dav, dag, sbk16, sbv16, dak16, dav16) = _inproj(
            hs, scale_s, shift_s, g_l, w_in16[l], *tabs_s, tm=tm_s, rows_per_group=None)
        new_s.append((sbk, sbv, dak, dav))
        s3 = lambda a: a.reshape(db, dseq, GROUP_WIDTH)
        da_rows = lambda a: a.reshape(db, dseq * DA_HEADS, 2 * HEAD_DIM)
        m_sb, m_da = _sample_attention(
            page_table, (s3(sbq), s3(sbk16), s3(sbv16), s3(sbg), s3(daq), da_rows(dak16), da_rows(dav16),
                         s3(dag)),
            caches, l, lam_all[l], subln, lam_init=lam_init)
        hs = _outproj(hs, m_sb.reshape(rows_s, GROUP_WIDTH), m_da.reshape(rows_s, GROUP_WIDTH),
                      w_out16[l], gate_s, fg, tm=tm_s, rows_per_group=None, final=final)

    def stack(rows, idx, lead, heads):
        return jnp.stack([r[idx] for r in rows], axis=0).reshape((depth,) + lead + heads)

    sb_heads = (SB_HEADS, HEAD_DIM)
    da_heads = (DA_HEADS, 2 * HEAD_DIM)
    outs_p = [stack(new_p, i, (bsz, seq), sb_heads if i < 2 else da_heads) for i in range(4)]
    outs_s = [stack(new_s, i, (db, dseq), sb_heads if i < 2 else da_heads) for i in range(4)]
    return (hp.reshape(bsz, seq, d), hs.reshape(db, dseq, d), *outs_p, *outs_s)
```

```python
import functools
import math

import jax
import jax.numpy as jnp
from jax import lax
from jax.experimental import pallas as pl
from jax.experimental.pallas import tpu as pltpu

F32 = jnp.float32
BF16 = jnp.bfloat16

HEAD_DIM = 64
LANES = 128
SLABS = 4
GROUP_WIDTH = SLABS * LANES
SB_HEADS = GROUP_WIDTH // HEAD_DIM
DA_HEADS = GROUP_WIDTH // (2 * HEAD_DIM)
ROPE_THETA = 10000.0
EPS = 1e-6
Q_SCALE = HEAD_DIM ** -0.5
LOG2E = math.log2(math.e)
EXP_ZERO_ABOVE = 104.0
VMEM_LIMIT = 48 * 1024 * 1024
SB_BLOCK = 256
DA_BLOCK = 512
PAGES_PER_STEP = 8
SB_PAGE_GROUP = 2


def _cparams(sem):
    return pltpu.CompilerParams(dimension_semantics=sem, vmem_limit_bytes=VMEM_LIMIT)


def _mod_kernel(c_ref, w_ref, b_ref, o_ref):
    c = c_ref[...]
    sc = c / (1.0 + jnp.exp(-c))
    o_ref[0, 0] = jnp.dot(sc, w_ref[0], precision=lax.Precision.HIGHEST,
                          preferred_element_type=F32) + b_ref[0, 0]


def _modulation(c_all, w_mod, b_mod):
    depth, d, _ = w_mod.shape
    r = c_all.shape[0]
    return pl.pallas_call(
        _mod_kernel,
        grid=(depth, 3),
        in_specs=[
            pl.BlockSpec((r, d), lambda l, j: (0, 0)),
            pl.BlockSpec((1, d, d), lambda l, j: (l, 0, j)),
            pl.BlockSpec((1, 1, 1, d), lambda l, j: (l, j, 0, 0)),
        ],
        out_specs=pl.BlockSpec((1, 1, r, d), lambda l, j: (l, j, 0, 0)),
        out_shape=jax.ShapeDtypeStruct((depth, 3, r, d), F32),
        compiler_params=_cparams(("arbitrary", "arbitrary")),
        name="modulation",
    )(c_all, w_mod, b_mod.reshape(depth, 3, 1, d))


def _inproj_kernel(x_ref, scale_ref, shift_ref, g_ref, w_ref, cos_ref, sin_up_ref, sin_dn_ref,
                   sbq_ref, sbk_ref, sbv_ref, sbg_ref, daq_ref, dak_ref, dav_ref, dag_ref,
                   sbk16_ref, sbv16_ref, dak16_ref, dav16_ref, *, sb_transposed):
    x = x_ref[...]
    tm = x.shape[0]
    xn = x * lax.rsqrt(jnp.mean(x * x, axis=-1, keepdims=True) + EPS)
    h = (xn * g_ref[...]) * (1.0 + scale_ref[...]) + shift_ref[...]
    h16 = h.astype(BF16)

    def seg(i):
        return jnp.dot(h16, w_ref[:, i * GROUP_WIDTH:(i + 1) * GROUP_WIDTH],
                       preferred_element_type=F32)

    def rope_slab(p):
        return (p * cos_ref[...] + pltpu.roll(p, LANES - HEAD_DIM // 2, 1) * sin_up_ref[...]
                + pltpu.roll(p, HEAD_DIM // 2, 1) * sin_dn_ref[...])

    sbq_ref[...] = (seg(0) * Q_SCALE).astype(BF16)
    k = seg(1)
    if sb_transposed:
        sbk_ref[0] = k.T
    else:
        sbk_ref[...] = k
    sbk16_ref[...] = k.astype(BF16)
    v = seg(2)
    if sb_transposed:
        sbv_ref[0] = v.T
    else:
        sbv_ref[...] = v
    sbv16_ref[...] = v.astype(BF16)
    g = seg(3)
    sbg_ref[...] = g / (1.0 + jnp.exp(-g))
    q = seg(4)
    for s in range(SLABS):
        sl = slice(s * LANES, (s + 1) * LANES)
        daq_ref[:, sl] = (rope_slab(q[:, sl]) * (Q_SCALE * LOG2E)).astype(BF16)
    k = seg(5)
    for s in range(SLABS):
        sl = slice(s * LANES, (s + 1) * LANES)
        kr = rope_slab(k[:, sl])
        dak_ref[pl.ds(s, tm, stride=SLABS), :] = kr
        dak16_ref[:, sl] = kr.astype(BF16)
    v = seg(6)
    for s in range(SLABS):
        dav_ref[pl.ds(s, tm, stride=SLABS), :] = v[:, s * LANES:(s + 1) * LANES]
    dav16_ref[...] = v.astype(BF16)
    g = seg(7)
    dag_ref[...] = g / (1.0 + jnp.exp(-g))


def _inproj(x, scale, shift, norm_g, w16, cos_t, sin_up, sin_dn, *, tm, rows_per_group):
    rows, d = x.shape
    nblk = rows // tm
    if rows_per_group is None:
        mod_spec = pl.BlockSpec((tm, d), lambda i: (i, 0))
    else:
        per = rows_per_group // tm
        mod_spec = pl.BlockSpec((None, 1, d), lambda i: (i // per, 0, 0))
    tab_blocks = cos_t.shape[0] // tm
    tab_spec = pl.BlockSpec((tm, LANES), lambda i: (i % tab_blocks, 0))
    out_spec = pl.BlockSpec((tm, GROUP_WIDTH), lambda i: (i, 0))
    f32o = jax.ShapeDtypeStruct((rows, GROUP_WIDTH), F32)
    b16o = jax.ShapeDtypeStruct((rows, GROUP_WIDTH), BF16)
    da_spec = pl.BlockSpec((tm * SLABS, LANES), lambda i: (i, 0))
    da_o = jax.ShapeDtypeStruct((rows * SLABS, LANES), F32)
    if rows_per_group is None:
        sb_spec, sb_o = out_spec, f32o
    else:
        sb_spec = pl.BlockSpec((1, GROUP_WIDTH, tm), lambda i: (i // per, 0, i % per))
        sb_o = jax.ShapeDtypeStruct((rows // rows_per_group, GROUP_WIDTH, rows_per_group), F32)
    return pl.pallas_call(
        functools.partial(_inproj_kernel, sb_transposed=rows_per_group is not None),
        grid=(nblk,),
        in_specs=[
            pl.BlockSpec((tm, d), lambda i: (i, 0)),
            mod_spec, mod_spec,
            pl.BlockSpec((1, d), lambda i: (0, 0)),
            pl.BlockSpec(w16.shape, lambda i: (0, 0)),
            tab_spec, tab_spec, tab_spec,
        ],
        out_specs=[out_spec, sb_spec, sb_spec, out_spec, out_spec, da_spec, da_spec, out_spec] + [out_spec] * 4,
        out_shape=[b16o, sb_o, sb_o, f32o, b16o, da_o, da_o, f32o, b16o, b16o, b16o, b16o],
        compiler_params=_cparams(("arbitrary",)),
        name="inproj",
    )(x, scale, shift, norm_g, w16, cos_t, sin_up, sin_dn)


def _nt_dot(a, b):
    return lax.dot_general(a, b, (((1,), (1,)), ((), ())), preferred_element_type=F32)


def _suffix_matrix(tk):
    j = lax.broadcasted_iota(jnp.int32, (tk, tk), 0)
    s = lax.broadcasted_iota(jnp.int32, (tk, tk), 1)
    return jnp.where(j > s, 1.0, 0.0).astype(BF16)


def _sb_weights(z, run, valid, upper):
    e = jnp.log(1.0 + jnp.exp(-jnp.abs(z)))
    sp = jnp.maximum(z, 0.0) + e
    ls = jnp.minimum(z, 0.0) - e
    if valid is not None:
        sp = jnp.where(valid, sp, 0.0)
    sp_hi = sp.astype(BF16)
    sp_lo = (sp - sp_hi.astype(F32)).astype(BF16)
    after = (jnp.dot(sp_hi, upper, preferred_element_type=F32)
             + jnp.dot(sp_lo, upper, preferred_element_type=F32))
    a = jnp.exp(ls - (after + run))
    if valid is not None:
        a = jnp.where(valid, a, 0.0)
    return a.astype(BF16), run + jnp.sum(sp, axis=-1, keepdims=True)


def _lane_chunks(x):
    return [x[:, c * LANES:(c + 1) * LANES] for c in range(x.shape[1] // LANES)]


def _da_weights(tiles, m, l, valid):
    if valid is not None:
        tiles = [jnp.where(valid, s, -jnp.inf) for s in tiles]
    chunks = [_lane_chunks(s) for s in tiles]
    top = chunks[0][0]
    for c in [c for cs in chunks for c in cs][1:]:
        top = jnp.maximum(top, c)
    m_new = jnp.maximum(m, jnp.broadcast_to(jnp.max(top, axis=-1, keepdims=True), m.shape))
    alpha = jnp.exp2(m - m_new)
    l_new = alpha * l
    ps = []
    for cs in chunks:
        pcs = [jnp.exp2(c - m_new) for c in cs]
        for pc in pcs:
            l_new = l_new + pc
        ps.append(jnp.concatenate([pc.astype(BF16) for pc in pcs], axis=1))
    return m_new, alpha, ps, l_new


def _lambda(lam_ref, lam_init):
    lam = lam_ref[...]
    s1 = jnp.sum(lam[0:1] * lam[1:2], axis=-1, keepdims=True)
    s2 = jnp.sum(lam[2:3] * lam[3:4], axis=-1, keepdims=True)
    return jnp.exp(s1) - jnp.exp(s2) + lam_init


def _da_finish(acc1, l1, acc2, l2, lam, subln_g, lam_init, gate):
    o = (acc1 / jnp.sum(l1, axis=-1, keepdims=True)
         - lam * (acc2 / jnp.sum(l2, axis=-1, keepdims=True)))
    on = o * lax.rsqrt(jnp.mean(o * o, axis=-1, keepdims=True) + EPS)
    return ((on * subln_g) * (1.0 - lam_init) * gate).astype(BF16)


def _stack_queries(q):
    lane = lax.broadcasted_iota(jnp.int32, q.shape, 1)
    zero = jnp.zeros_like(q)
    return jnp.concatenate([jnp.where(lane < HEAD_DIM, q, zero),
                            jnp.where(lane < HEAD_DIM, zero, q)], axis=0)


def _row_col(m, tk, tq):
    row = lax.broadcasted_iota(jnp.int32, (m, tk), 0)
    col = lax.broadcasted_iota(jnp.int32, (m, tk), 1)
    return jnp.where(row >= tq, row - tq, row), col


def _sb_prompt_kernel(q_ref, k_ref, v_ref, g_ref, o_ref, *, blk):
    i = pl.program_id(2)
    q2 = _stack_queries(q_ref[0])
    upper = _suffix_matrix(blk)
    t, s = _row_col(2 * blk, blk, blk)

    def kv(j):
        start = pl.multiple_of(j * blk, blk)
        return k_ref[0, pl.ds(start, blk), :], v_ref[0, pl.ds(start, blk), :]

    def block(j, run, valid):
        k, v = kv(j)
        a, run = _sb_weights(_nt_dot(q2, k), run, valid, upper)
        return jnp.dot(a, v, preferred_element_type=F32), run

    def pair(j, run, valid_hi, valid_lo):
        pv_hi, run = block(j, run, valid_hi)
        pv_lo, run = block(jnp.maximum(j - 1, 0), run, valid_lo)
        return pv_hi + pv_lo, run

    def live(run):
        return jnp.min(run) < EXP_ZERO_ABOVE

    acc, run = pair(i, jnp.zeros((2 * blk, 1), F32), s < t, s >= jnp.where(i >= 1, 0, blk))

    def cond(c):
        j, _, run = c
        return jnp.logical_and(j >= 1, live(run))

    def body(c):
        j, acc, run = c
        pv, run = pair(j, run, None, None)
        return j - 2, acc + pv, run

    j, acc, run = lax.while_loop(cond, body, (i - 2, acc, run))

    def last_block(acc, run):
        pv, run = block(0, run, None)
        return acc + pv, run

    acc, _ = lax.cond(jnp.logical_and(j == 0, live(run)), last_block, lambda acc, run: (acc, run), acc, run)
    lane = lax.broadcasted_iota(jnp.int32, (blk, LANES), 1)
    o = jnp.where(lane < HEAD_DIM, acc[:blk], acc[blk:])
    o_ref[0] = (o * g_ref[0]).astype(BF16)


def _da_prompt_kernel(q_ref, k_ref, v_ref, g_ref, lam_ref, subln_ref, o_ref,
                      s_even, s_odd, m_s, l_s, acc_s, *, blk, lam_init):
    i = pl.program_id(2)
    q2 = _stack_queries(q_ref[0])
    m2 = 2 * blk

    def scores_into(s_ref, j):
        start = pl.multiple_of(j * blk, blk)
        s_ref[...] = _nt_dot(q2, k_ref[0, pl.ds(start, blk), :])

    def absorb(s_ref, j, valid):
        start = pl.multiple_of(j * blk, blk)
        m, alpha, (p,), l = _da_weights([s_ref[...]], m_s[...], l_s[...], valid)
        m_s[...] = m
        l_s[...] = l
        acc_s[...] = alpha * acc_s[...] + jnp.dot(p, v_ref[0, pl.ds(start, blk), :],
                                                  preferred_element_type=F32)

    m_s[...] = jnp.full((m2, LANES), -jnp.inf, F32)
    l_s[...] = jnp.zeros((m2, LANES), F32)
    acc_s[...] = jnp.zeros((m2, LANES), F32)
    scores_into(s_even, 0)

    def pair(jj, carry):
        j = 2 * jj
        scores_into(s_odd, j + 1)
        absorb(s_even, j, None)
        scores_into(s_even, j + 2)
        absorb(s_odd, j + 1, None)
        return carry

    lax.fori_loop(0, i // 2, pair, 0)
    t, s = _row_col(m2, blk, blk)
    diag = s <= t

    @pl.when(i % 2 == 0)
    def _():
        absorb(s_even, i, diag)

    @pl.when(i % 2 == 1)
    def _():
        scores_into(s_odd, i)
        absorb(s_even, i - 1, None)
        absorb(s_odd, i, diag)

    acc, l = acc_s[...], l_s[...]
    o_ref[0] = _da_finish(acc[:blk], l[:blk], acc[blk:], l[blk:], _lambda(lam_ref, lam_init),
                          subln_ref[...], lam_init, g_ref[0])


def _prompt_attention(kernel, name, q, k, v, g, extra, scratch, *, blk):
    b, t, _ = q.shape
    qspec = pl.BlockSpec((1, blk, LANES), lambda bi, s, i: (bi, i, s))
    kvspec = pl.BlockSpec((1, t, LANES), lambda bi, s, i: (bi, 0, s))
    extra_specs = [pl.BlockSpec(e.shape, lambda bi, s, i: (0, 0)) for e in extra]
    return pl.pallas_call(
        kernel,
        grid=(b, SLABS, t // blk),
        in_specs=[qspec, kvspec, kvspec, qspec] + extra_specs,
        out_specs=qspec,
        out_shape=jax.ShapeDtypeStruct((b, t, GROUP_WIDTH), BF16),
        scratch_shapes=scratch,
        compiler_params=_cparams(("arbitrary", "arbitrary", "arbitrary")),
        name=name,
    )(q, k, v, g, *extra)


def _sample_kernel(pt_ref, sbq_ref, sbkn_ref, sbvn_ref, sbg_ref, daq_ref, dakn_ref, davn_ref, dag_ref,
                   lam_ref, subln_ref, *rest, page, tq, n_step_pages, lam_init):
    del pt_ref
    pages = rest[:4 * n_step_pages]
    osb_ref, oda_ref = rest[4 * n_step_pages:4 * n_step_pages + 2]
    qbd_s, q2_s, sb_acc, sb_run, da_m, da_l, da_acc = rest[4 * n_step_pages + 2:]
    step = pl.program_id(1)
    last = pl.num_programs(1) - 1
    n_sb = SB_HEADS * tq
    n_da = 2 * DA_HEADS * tq
    upper = _suffix_matrix(page)

    def iota(shape, axis):
        return lax.broadcasted_iota(jnp.int32, shape, axis)

    da_head_of_row = iota((n_da, DA_HEADS * page), 0) // (2 * tq)
    da_page_valid = (iota((n_da, DA_HEADS * page), 1) % DA_HEADS) == da_head_of_row

    def pad_rows(x):
        return jnp.concatenate([x, jnp.zeros((page - x.shape[0], x.shape[1]), x.dtype)], axis=0)

    @pl.when(step == 0)
    def _():
        q = sbq_ref[0].astype(F32)
        q_rows = jnp.concatenate([q] * SB_HEADS, axis=0)
        own = (iota((n_sb, GROUP_WIDTH), 1) // HEAD_DIM) == (iota((n_sb, GROUP_WIDTH), 0) // tq)
        qbd = jnp.where(own, q_rows, 0.0).astype(BF16)
        qbd_s[...] = qbd
        q = daq_ref[0].astype(F32)
        lane = iota((tq, LANES), 1)
        pieces = []
        for h in range(DA_HEADS):
            slab = q[:, h * LANES:(h + 1) * LANES]
            pieces += [jnp.where(lane < HEAD_DIM, slab, 0.0), jnp.where(lane < HEAD_DIM, 0.0, slab)]
        q2 = jnp.concatenate(pieces, axis=0).astype(BF16)
        q2_s[...] = q2

        t_sb = iota((n_sb, page), 0) % tq
        a, run = _sb_weights(_nt_dot(qbd, pad_rows(sbkn_ref[0])), jnp.zeros((n_sb, 1), F32),
                             iota((n_sb, page), 1) < t_sb, upper)
        sb_acc[...] = jnp.dot(a, pad_rows(sbvn_ref[0]), preferred_element_type=F32)
        sb_run[...] = run

        c = iota((n_da, page), 1)
        r = iota((n_da, page), 0)
        valid = jnp.logical_and((c % DA_HEADS) == r // (2 * tq), c // DA_HEADS <= r % tq)
        m, _, (p,), l = _da_weights([_nt_dot(q2, pad_rows(dakn_ref[0]))], jnp.full((n_da, LANES), -jnp.inf, F32),
                                    jnp.zeros((n_da, LANES), F32), valid)
        da_m[...] = m
        da_l[...] = l
        da_acc[...] = jnp.dot(p, pad_rows(davn_ref[0]), preferred_element_type=F32)

    def sb_groups(g0):
        if g0 >= n_step_pages:
            return

        @pl.when(jnp.min(sb_run[...]) < EXP_ZERO_ABOVE)
        def _():
            run = sb_run[...]
            acc = sb_acc[...]
            for i in range(g0, min(g0 + SB_PAGE_GROUP, n_step_pages)):
                kt = pages[4 * i][0, 0].astype(BF16)
                vt = pages[4 * i + 1][0, 0].astype(BF16)
                a, run = _sb_weights(jnp.dot(qbd_s[...], kt, preferred_element_type=F32), run, None, upper)
                acc = acc + _nt_dot(a, vt)
            sb_acc[...] = acc
            sb_run[...] = run
            sb_groups(g0 + SB_PAGE_GROUP)

    sb_groups(0)

    tiles = [_nt_dot(q2_s[...], pages[4 * i + 2][0, 0].astype(BF16))
             for i in range(n_step_pages)]
    m, alpha, ps, l = _da_weights(tiles, da_m[...], da_l[...], da_page_valid)
    acc = alpha * da_acc[...]
    for i, p in enumerate(ps):
        acc = acc + jnp.dot(p, pages[4 * i + 3][0, 0].astype(BF16), preferred_element_type=F32)
    da_m[...] = m
    da_l[...] = l
    da_acc[...] = acc

    @pl.when(step == last)
    def _():
        acc = sb_acc[...]
        col_head = iota((tq, GROUP_WIDTH), 1) // HEAD_DIM
        o = jnp.zeros((tq, GROUP_WIDTH), F32)
        for h in range(SB_HEADS):
            o = o + jnp.where(col_head == h, acc[h * tq:(h + 1) * tq], 0.0)
        osb_ref[0] = (o * sbg_ref[0]).astype(BF16)

        lam = _lambda(lam_ref, lam_init)
        acc, l = da_acc[...], da_l[...]
        for h in range(DA_HEADS):
            r1, r2 = 2 * h * tq, (2 * h + 1) * tq
            oda_ref[0, :, h * LANES:(h + 1) * LANES] = _da_finish(
                acc[r1:r1 + tq], l[r1:r1 + tq], acc[r2:r2 + tq], l[r2:r2 + tq], lam, subln_ref[...],
                lam_init, dag_ref[0, :, h * LANES:(h + 1) * LANES])


def _sample_attention(page_table, new, caches, layer, lam_vecs, subln_g, *, lam_init):
    sbq = new[0]
    db, tq, _ = sbq.shape
    n_pages = page_table.shape[1]
    page = caches[0].shape[3]
    n_step_pages = math.gcd(PAGES_PER_STEP, n_pages)

    def new_spec(a):
        return pl.BlockSpec((1,) + a.shape[1:], lambda b, st, pt: (b, 0, 0))

    def page_spec(i):
        def index(b, st, pt):
            return (layer, pt[b, n_pages - 1 - (st * n_step_pages + i)], 0, 0)
        return pl.BlockSpec((1, 1, GROUP_WIDTH, page), index)

    small = lambda e: pl.BlockSpec(e.shape, lambda b, st, pt: (0, 0))
    page_specs, page_args = [], []
    for i in range(n_step_pages):
        page_specs += [page_spec(i)] * 4
        page_args += list(caches)
    n_sb, n_da = SB_HEADS * tq, 2 * DA_HEADS * tq
    out = jax.ShapeDtypeStruct((db, tq, GROUP_WIDTH), BF16)
    out_spec = pl.BlockSpec((1, tq, GROUP_WIDTH), lambda b, st, pt: (b, 0, 0))
    grid_spec = pltpu.PrefetchScalarGridSpec(
        num_scalar_prefetch=1,
        grid=(db, n_pages // n_step_pages),
        in_specs=[new_spec(a) for a in new] + [small(lam_vecs), small(subln_g)] + page_specs,
        out_specs=[out_spec, out_spec],
        scratch_shapes=[
            pltpu.VMEM((n_sb, GROUP_WIDTH), BF16), pltpu.VMEM((n_da, LANES), BF16),
            pltpu.VMEM((n_sb, GROUP_WIDTH), F32), pltpu.VMEM((n_sb, 1), F32),
            pltpu.VMEM((n_da, LANES), F32), pltpu.VMEM((n_da, LANES), F32), pltpu.VMEM((n_da, LANES), F32),
        ],
    )
    return pl.pallas_call(
        functools.partial(_sample_kernel, page=page, tq=tq, n_step_pages=n_step_pages, lam_init=lam_init),
        grid_spec=grid_spec,
        out_shape=[out, out],
        compiler_params=_cparams(("arbitrary", "arbitrary")),
        name="sample_attention",
    )(page_table, *new, lam_vecs, subln_g, *page_args)


def _outproj_kernel(x_ref, msb_ref, mda_ref, w_ref, gate_ref, fg_ref, o_ref, *, final):
    half = msb_ref.shape[1]
    y = (jnp.dot(msb_ref[...], w_ref[:half], preferred_element_type=F32)
         + jnp.dot(mda_ref[...], w_ref[half:], preferred_element_type=F32))
    y = x_ref[...] + gate_ref[...] * y
    if final:
        y = y * lax.rsqrt(jnp.mean(y * y, axis=-1, keepdims=True) + EPS) * fg_ref[...]
    o_ref[...] = y


def _outproj(x, m_sb, m_da, w16, gate, final_g, *, tm, rows_per_group, final):
    rows, d = x.shape
    if rows_per_group is None:
        gate_spec = pl.BlockSpec((tm, d), lambda i: (i, 0))
    else:
        per = rows_per_group // tm
        gate_spec = pl.BlockSpec((None, 1, d), lambda i: (i // per, 0, 0))
    row_spec = pl.BlockSpec((tm, d), lambda i: (i, 0))
    mix_spec = pl.BlockSpec((tm, GROUP_WIDTH), lambda i: (i, 0))
    return pl.pallas_call(
        functools.partial(_outproj_kernel, final=final),
        grid=(rows // tm,),
        in_specs=[row_spec, mix_spec, mix_spec, pl.BlockSpec(w16.shape, lambda i: (0, 0)),
                  gate_spec, pl.BlockSpec((1, d), lambda i: (0, 0))],
        out_specs=row_spec,
        out_shape=jax.ShapeDtypeStruct((rows, d), F32),
        compiler_params=_cparams(("arbitrary",)),
        name="outproj",
    )(x, m_sb, m_da, w16, gate, final_g)


def _rope_tables(pos):
    inv_freq = ROPE_THETA ** (-jnp.arange(0, HEAD_DIM, 2, dtype=F32) / HEAD_DIM)
    ang = pos.astype(F32)[:, None] * inv_freq[None, :]
    ang = jnp.concatenate([ang, ang, ang, ang], axis=-1)
    first_half = (jnp.arange(LANES) % HEAD_DIM) < HEAD_DIM // 2
    sin = jnp.sin(ang)
    return jnp.cos(ang), jnp.where(first_half, -sin, 0.0), jnp.where(first_half, 0.0, sin)


def _pick_tile(rows, target):
    tm = min(rows, target)
    while rows % tm:
        tm //= 2
    return tm


def kernel(x_prompt, x_sample, c_prompt, c_sample, cache_sb_k, cache_sb_v, cache_da_k, cache_da_v,
           page_table, w_mod, b_mod, norm_g, w_in, w_out, lambda_q1, lambda_k1, lambda_q2, lambda_k2,
           subln_g, final_norm_g):
    bsz, seq, d = x_prompt.shape
    db, dseq, _ = x_sample.shape
    depth = w_in.shape[0]
    n_pool, page = cache_sb_k.shape[1], cache_sb_k.shape[2]
    n_pages = page_table.shape[1]
    past_len = n_pages * page

    n_c = bsz + db
    c_all = jnp.concatenate([c_prompt, c_sample, jnp.zeros((-n_c % 8, d), F32)], axis=0)
    mod = _modulation(c_all, w_mod, b_mod)

    tabs_p = _rope_tables(jnp.arange(seq, dtype=jnp.int32))
    tabs_s = tuple(jnp.tile(t, (db, 1))
                   for t in _rope_tables(past_len + jnp.arange(dseq, dtype=jnp.int32)))

    caches = tuple(jnp.transpose(c, (0, 1, 3, 4, 2)).reshape(depth, n_pool, GROUP_WIDTH, page)
                   for c in (cache_sb_k, cache_sb_v))
    caches += tuple(c.reshape(depth, n_pool, page * DA_HEADS, 2 * HEAD_DIM)
                    for c in (cache_da_k, cache_da_v))
    w_in16 = w_in.astype(BF16)
    w_out16 = w_out.astype(BF16)
    lam_all = jnp.stack([lambda_q1, lambda_k1, lambda_q2, lambda_k2], axis=1)
    fg = final_norm_g.reshape(1, d)

    rows_p, rows_s = bsz * seq, db * dseq
    tm_p = _pick_tile(seq, 256)
    tm_s = _pick_tile(rows_s, 256)
    sb_blk = _pick_tile(seq, SB_BLOCK)
    da_blk = _pick_tile(seq, DA_BLOCK)

    hp = x_prompt.reshape(rows_p, d)
    hs = x_sample.reshape(rows_s, d)
    new_p, new_s = [], []
    for l in range(depth):
        lam_init = 0.8 - 0.6 * math.exp(-0.3 * l)
        g_l = norm_g[l].reshape(1, d)
        subln = subln_g[l].reshape(1, LANES)
        final = l == depth - 1

        def per_row(v):
            return jnp.broadcast_to(v[:, None, :], (db, dseq, d)).reshape(rows_s, d)

        shift_p, scale_p, gate_p = (mod[l, j, :bsz].reshape(bsz, 1, d) for j in range(3))
        shift_s, scale_s, gate_s = (per_row(mod[l, j, bsz:n_c]) for j in range(3))

        (sbq, sbk, sbv, sbg, daq, dak, dav, dag, sbk16, sbv16, dak16, dav16) = _inproj(
            hp, scale_p, shift_p, g_l, w_in16[l], *tabs_p, tm=tm_p, rows_per_group=seq)
        new_p.append((sbk, sbv, dak, dav))
        r3 = lambda a: a.reshape(bsz, seq, GROUP_WIDTH)
        m_sb = _prompt_attention(functools.partial(_sb_prompt_kernel, blk=sb_blk), "sb_prompt",
                                 r3(sbq), r3(sbk16), r3(sbv16), r3(sbg), (), [], blk=sb_blk)
        da_scratch = [pltpu.VMEM((2 * da_blk, da_blk), F32), pltpu.VMEM((2 * da_blk, da_blk), F32),
                      pltpu.VMEM((2 * da_blk, LANES), F32), pltpu.VMEM((2 * da_blk, LANES), F32),
                      pltpu.VMEM((2 * da_blk, LANES), F32)]
        m_da = _prompt_attention(functools.partial(_da_prompt_kernel, blk=da_blk, lam_init=lam_init),
                                 "da_prompt", r3(daq), r3(dak16), r3(dav16), r3(dag),
                                 (lam_all[l], subln), da_scratch, blk=da_blk)
        hp = _outproj(hp, m_sb.reshape(rows_p, GROUP_WIDTH), m_da.reshape(rows_p, GROUP_WIDTH),
                      w_out16[l], gate_p, fg, tm=tm_p, rows_per_group=seq, final=final)

        (sbq, sbk, sbv, sbg, daq, dak, dav, dag, sbk16, sbv16, dak16, dav16) = _inproj(
            hs, scale_s, shift_s, g_l, w_in16[l], *tabs_s, tm=tm_s, rows_per_group=None)
        new_s.append((sbk, sbv, dak, dav))
        s3 = lambda a: a.reshape(db, dseq, GROUP_WIDTH)
        da_rows = lambda a: a.reshape(db, dseq * DA_HEADS, 2 * HEAD_DIM)
        m_sb, m_da = _sample_attention(
            page_table, (s3(sbq), s3(sbk16), s3(sbv16), s3(sbg), s3(daq), da_rows(dak16), da_rows(dav16),
                         s3(dag)),
            caches, l, lam_all[l], subln, lam_init=lam_init)
        hs = _outproj(hs, m_sb.reshape(rows_s, GROUP_WIDTH), m_da.reshape(rows_s, GROUP_WIDTH),
                      w_out16[l], gate_s, fg, tm=tm_s, rows_per_group=None, final=final)

    def stacked(rows, idx):
        return jnp.stack([r[idx] for r in rows], axis=0)

    sb_heads = (SB_HEADS, HEAD_DIM)
    da_heads = (DA_HEADS, 2 * HEAD_DIM)
    outs_p = [jnp.transpose(stacked(new_p, i).reshape((depth, bsz) + sb_heads + (seq,)), (0, 1, 4, 2, 3))
              for i in range(2)]
    outs_p += [stacked(new_p, i).reshape((depth, bsz, seq) + da_heads) for i in range(2, 4)]
    outs_s = [stacked(new_s, i).reshape((depth, db, dseq) + (sb_heads if i < 2 else da_heads))
              for i in range(4)]
    return (hp.reshape(bsz, seq, d), hs.reshape(db, dseq, d), *outs_p, *outs_s)
```

```python
import functools
import math

import jax
import jax.numpy as jnp
from jax import lax
from jax.experimental import pallas as pl
from jax.experimental.pallas import tpu as pltpu

F32 = jnp.float32
BF16 = jnp.bfloat16

HEAD_DIM = 64
LANES = 128
SLABS = 4
GROUP_WIDTH = SLABS * LANES
SB_HEADS = GROUP_WIDTH // HEAD_DIM
DA_HEADS = GROUP_WIDTH // (2 * HEAD_DIM)
ROPE_THETA = 10000.0
EPS = 1e-6
Q_SCALE = HEAD_DIM ** -0.5
LOG2E = math.log2(math.e)
EXP_ZERO_ABOVE = 104.0
VMEM_LIMIT = 48 * 1024 * 1024
SB_BLOCK = 256
DA_BLOCK = 512
PAGES_PER_STEP = 16
SB_PAGE_GROUP = 2


def _cparams(sem):
    return pltpu.CompilerParams(dimension_semantics=sem, vmem_limit_bytes=VMEM_LIMIT)


def _mod_kernel(c_ref, w_ref, b_ref, o_ref):
    c = c_ref[...]
    sc = c / (1.0 + jnp.exp(-c))
    o_ref[0, 0] = jnp.dot(sc, w_ref[0], precision=lax.Precision.HIGHEST,
                          preferred_element_type=F32) + b_ref[0, 0]


def _modulation(c_all, w_mod, b_mod):
    depth, d, _ = w_mod.shape
    r = c_all.shape[0]
    return pl.pallas_call(
        _mod_kernel,
        grid=(depth, 3),
        in_specs=[
            pl.BlockSpec((r, d), lambda l, j: (0, 0)),
            pl.BlockSpec((1, d, d), lambda l, j: (l, 0, j)),
            pl.BlockSpec((1, 1, 1, d), lambda l, j: (l, j, 0, 0)),
        ],
        out_specs=pl.BlockSpec((1, 1, r, d), lambda l, j: (l, j, 0, 0)),
        out_shape=jax.ShapeDtypeStruct((depth, 3, r, d), F32),
        compiler_params=_cparams(("arbitrary", "arbitrary")),
        name="modulation",
    )(c_all, w_mod, b_mod.reshape(depth, 3, 1, d))


def _inproj_kernel(x_ref, scale_ref, shift_ref, g_ref, w_ref, cos_ref, sin_up_ref, sin_dn_ref,
                   sbq_ref, sbk_ref, sbv_ref, sbg_ref, daq_ref, dak_ref, dav_ref, dag_ref,
                   sbk16_ref, sbv16_ref, dak16_ref, dav16_ref, *, sb_transposed):
    x = x_ref[...]
    tm = x.shape[0]
    xn = x * lax.rsqrt(jnp.mean(x * x, axis=-1, keepdims=True) + EPS)
    h = (xn * g_ref[...]) * (1.0 + scale_ref[...]) + shift_ref[...]
    h16 = h.astype(BF16)

    def seg(i):
        return jnp.dot(h16, w_ref[:, i * GROUP_WIDTH:(i + 1) * GROUP_WIDTH],
                       preferred_element_type=F32)

    def rope_slab(p):
        return (p * cos_ref[...] + pltpu.roll(p, LANES - HEAD_DIM // 2, 1) * sin_up_ref[...]
                + pltpu.roll(p, HEAD_DIM // 2, 1) * sin_dn_ref[...])

    sbq_ref[...] = (seg(0) * Q_SCALE).astype(BF16)
    k = seg(1)
    if sb_transposed:
        sbk_ref[0] = k.T
    else:
        sbk_ref[...] = k
    sbk16_ref[...] = k.astype(BF16)
    v = seg(2)
    if sb_transposed:
        sbv_ref[0] = v.T
    else:
        sbv_ref[...] = v
    sbv16_ref[...] = v.astype(BF16)
    g = seg(3)
    sbg_ref[...] = g / (1.0 + jnp.exp(-g))
    q = seg(4)
    for s in range(SLABS):
        sl = slice(s * LANES, (s + 1) * LANES)
        daq_ref[:, sl] = (rope_slab(q[:, sl]) * (Q_SCALE * LOG2E)).astype(BF16)
    k = seg(5)
    for s in range(SLABS):
        sl = slice(s * LANES, (s + 1) * LANES)
        kr = rope_slab(k[:, sl])
        dak_ref[pl.ds(s, tm, stride=SLABS), :] = kr
        dak16_ref[:, sl] = kr.astype(BF16)
    v = seg(6)
    for s in range(SLABS):
        dav_ref[pl.ds(s, tm, stride=SLABS), :] = v[:, s * LANES:(s + 1) * LANES]
    dav16_ref[...] = v.astype(BF16)
    g = seg(7)
    dag_ref[...] = g / (1.0 + jnp.exp(-g))


def _inproj(x, scale, shift, norm_g, w16, cos_t, sin_up, sin_dn, *, tm, rows_per_group):
    rows, d = x.shape
    nblk = rows // tm
    if rows_per_group is None:
        mod_spec = pl.BlockSpec((tm, d), lambda i: (i, 0))
    else:
        per = rows_per_group // tm
        mod_spec = pl.BlockSpec((None, 1, d), lambda i: (i // per, 0, 0))
    tab_blocks = cos_t.shape[0] // tm
    tab_spec = pl.BlockSpec((tm, LANES), lambda i: (i % tab_blocks, 0))
    out_spec = pl.BlockSpec((tm, GROUP_WIDTH), lambda i: (i, 0))
    f32o = jax.ShapeDtypeStruct((rows, GROUP_WIDTH), F32)
    b16o = jax.ShapeDtypeStruct((rows, GROUP_WIDTH), BF16)
    da_spec = pl.BlockSpec((tm * SLABS, LANES), lambda i: (i, 0))
    da_o = jax.ShapeDtypeStruct((rows * SLABS, LANES), F32)
    if rows_per_group is None:
        sb_spec, sb_o = out_spec, f32o
    else:
        sb_spec = pl.BlockSpec((1, GROUP_WIDTH, tm), lambda i: (i // per, 0, i % per))
        sb_o = jax.ShapeDtypeStruct((rows // rows_per_group, GROUP_WIDTH, rows_per_group), F32)
    return pl.pallas_call(
        functools.partial(_inproj_kernel, sb_transposed=rows_per_group is not None),
        grid=(nblk,),
        in_specs=[
            pl.BlockSpec((tm, d), lambda i: (i, 0)),
            mod_spec, mod_spec,
            pl.BlockSpec((1, d), lambda i: (0, 0)),
            pl.BlockSpec(w16.shape, lambda i: (0, 0)),
            tab_spec, tab_spec, tab_spec,
        ],
        out_specs=[out_spec, sb_spec, sb_spec, out_spec, out_spec, da_spec, da_spec, out_spec] + [out_spec] * 4,
        out_shape=[b16o, sb_o, sb_o, f32o, b16o, da_o, da_o, f32o, b16o, b16o, b16o, b16o],
        compiler_params=_cparams(("arbitrary",)),
        name="inproj",
    )(x, scale, shift, norm_g, w16, cos_t, sin_up, sin_dn)


def _nt_dot(a, b):
    return lax.dot_general(a, b, (((1,), (1,)), ((), ())), preferred_element_type=F32)


def _suffix_matrix(tk):
    j = lax.broadcasted_iota(jnp.int32, (tk, tk), 0)
    s = lax.broadcasted_iota(jnp.int32, (tk, tk), 1)
    return jnp.where(j > s, 1.0, 0.0).astype(BF16)


def _sb_weights(z, run, valid, upper):
    e = jnp.log(1.0 + jnp.exp(-jnp.abs(z)))
    sp = jnp.maximum(z, 0.0) + e
    ls = jnp.minimum(z, 0.0) - e
    if valid is not None:
        sp = jnp.where(valid, sp, 0.0)
    sp_hi = sp.astype(BF16)
    sp_lo = (sp - sp_hi.astype(F32)).astype(BF16)
    after = (jnp.dot(sp_hi, upper, preferred_element_type=F32)
             + jnp.dot(sp_lo, upper, preferred_element_type=F32))
    a = jnp.exp(ls - (after + run))
    if valid is not None:
        a = jnp.where(valid, a, 0.0)
    return a.astype(BF16), run + jnp.sum(sp, axis=-1, keepdims=True)


def _lane_chunks(x):
    return [x[:, c * LANES:(c + 1) * LANES] for c in range(x.shape[1] // LANES)]


def _da_weights(tiles, m, l, valid):
    if valid is not None:
        tiles = [jnp.where(valid, s, -jnp.inf) for s in tiles]
    chunks = [_lane_chunks(s) for s in tiles]
    top = chunks[0][0]
    for c in [c for cs in chunks for c in cs][1:]:
        top = jnp.maximum(top, c)
    m_new = jnp.maximum(m, jnp.broadcast_to(jnp.max(top, axis=-1, keepdims=True), m.shape))
    alpha = jnp.exp2(m - m_new)
    l_new = alpha * l
    ps = []
    for cs in chunks:
        pcs = [jnp.exp2(c - m_new) for c in cs]
        for pc in pcs:
            l_new = l_new + pc
        ps.append(jnp.concatenate([pc.astype(BF16) for pc in pcs], axis=1))
    return m_new, alpha, ps, l_new


def _lambda(lam_ref, lam_init):
    lam = lam_ref[...]
    s1 = jnp.sum(lam[0:1] * lam[1:2], axis=-1, keepdims=True)
    s2 = jnp.sum(lam[2:3] * lam[3:4], axis=-1, keepdims=True)
    return jnp.exp(s1) - jnp.exp(s2) + lam_init


def _da_finish(acc1, l1, acc2, l2, lam, subln_g, lam_init, gate):
    o = (acc1 / jnp.sum(l1, axis=-1, keepdims=True)
         - lam * (acc2 / jnp.sum(l2, axis=-1, keepdims=True)))
    on = o * lax.rsqrt(jnp.mean(o * o, axis=-1, keepdims=True) + EPS)
    return ((on * subln_g) * (1.0 - lam_init) * gate).astype(BF16)


def _stack_queries(q):
    lane = lax.broadcasted_iota(jnp.int32, q.shape, 1)
    zero = jnp.zeros_like(q)
    return jnp.concatenate([jnp.where(lane < HEAD_DIM, q, zero),
                            jnp.where(lane < HEAD_DIM, zero, q)], axis=0)


def _row_col(m, tk, tq):
    row = lax.broadcasted_iota(jnp.int32, (m, tk), 0)
    col = lax.broadcasted_iota(jnp.int32, (m, tk), 1)
    return jnp.where(row >= tq, row - tq, row), col


def _sb_prompt_kernel(q_ref, k_ref, v_ref, g_ref, o_ref, *, blk):
    i = pl.program_id(2)
    q2 = _stack_queries(q_ref[0])
    upper = _suffix_matrix(blk)
    t, s = _row_col(2 * blk, blk, blk)

    def kv(j):
        start = pl.multiple_of(j * blk, blk)
        return k_ref[0, pl.ds(start, blk), :], v_ref[0, pl.ds(start, blk), :]

    def block(j, run, valid):
        k, v = kv(j)
        a, run = _sb_weights(_nt_dot(q2, k), run, valid, upper)
        return jnp.dot(a, v, preferred_element_type=F32), run

    def pair(j, run, valid_hi, valid_lo):
        pv_hi, run = block(j, run, valid_hi)
        pv_lo, run = block(jnp.maximum(j - 1, 0), run, valid_lo)
        return pv_hi + pv_lo, run

    def live(run):
        return jnp.min(run) < EXP_ZERO_ABOVE

    acc, run = pair(i, jnp.zeros((2 * blk, 1), F32), s < t, s >= jnp.where(i >= 1, 0, blk))

    def cond(c):
        j, _, run = c
        return jnp.logical_and(j >= 1, live(run))

    def body(c):
        j, acc, run = c
        pv, run = pair(j, run, None, None)
        return j - 2, acc + pv, run

    j, acc, run = lax.while_loop(cond, body, (i - 2, acc, run))

    def last_block(acc, run):
        pv, run = block(0, run, None)
        return acc + pv, run

    acc, _ = lax.cond(jnp.logical_and(j == 0, live(run)), last_block, lambda acc, run: (acc, run), acc, run)
    lane = lax.broadcasted_iota(jnp.int32, (blk, LANES), 1)
    o = jnp.where(lane < HEAD_DIM, acc[:blk], acc[blk:])
    o_ref[0] = (o * g_ref[0]).astype(BF16)


def _da_prompt_kernel(q_ref, k_ref, v_ref, g_ref, lam_ref, subln_ref, o_ref,
                      s_even, s_odd, m_s, l_s, acc_s, *, blk, lam_init):
    i = pl.program_id(2)
    q2 = _stack_queries(q_ref[0])
    m2 = 2 * blk

    def scores_into(s_ref, j):
        start = pl.multiple_of(j * blk, blk)
        s_ref[...] = _nt_dot(q2, k_ref[0, pl.ds(start, blk), :])

    def absorb(s_ref, j, valid):
        start = pl.multiple_of(j * blk, blk)
        m, alpha, (p,), l = _da_weights([s_ref[...]], m_s[...], l_s[...], valid)
        m_s[...] = m
        l_s[...] = l
        acc_s[...] = alpha * acc_s[...] + jnp.dot(p, v_ref[0, pl.ds(start, blk), :],
                                                  preferred_element_type=F32)

    m_s[...] = jnp.full((m2, LANES), -jnp.inf, F32)
    l_s[...] = jnp.zeros((m2, LANES), F32)
    acc_s[...] = jnp.zeros((m2, LANES), F32)
    scores_into(s_even, 0)

    def pair(jj, carry):
        j = 2 * jj
        scores_into(s_odd, j + 1)
        absorb(s_even, j, None)
        scores_into(s_even, j + 2)
        absorb(s_odd, j + 1, None)
        return carry

    lax.fori_loop(0, i // 2, pair, 0)
    t, s = _row_col(m2, blk, blk)
    diag = s <= t

    @pl.when(i % 2 == 0)
    def _():
        absorb(s_even, i, diag)

    @pl.when(i % 2 == 1)
    def _():
        scores_into(s_odd, i)
        absorb(s_even, i - 1, None)
        absorb(s_odd, i, diag)

    acc, l = acc_s[...], l_s[...]
    o_ref[0] = _da_finish(acc[:blk], l[:blk], acc[blk:], l[blk:], _lambda(lam_ref, lam_init),
                          subln_ref[...], lam_init, g_ref[0])


def _prompt_attention(kernel, name, q, k, v, g, extra, scratch, *, blk):
    b, t, _ = q.shape
    qspec = pl.BlockSpec((1, blk, LANES), lambda bi, s, i: (bi, i, s))
    kvspec = pl.BlockSpec((1, t, LANES), lambda bi, s, i: (bi, 0, s))
    extra_specs = [pl.BlockSpec(e.shape, lambda bi, s, i: (0, 0)) for e in extra]
    return pl.pallas_call(
        kernel,
        grid=(b, SLABS, t // blk),
        in_specs=[qspec, kvspec, kvspec, qspec] + extra_specs,
        out_specs=qspec,
        out_shape=jax.ShapeDtypeStruct((b, t, GROUP_WIDTH), BF16),
        scratch_shapes=scratch,
        compiler_params=_cparams(("arbitrary", "arbitrary", "arbitrary")),
        name=name,
    )(q, k, v, g, *extra)


def _sample_kernel(pt_ref, sbq_ref, sbkn_ref, sbvn_ref, sbg_ref, daq_ref, dakn_ref, davn_ref, dag_ref,
                   lam_ref, subln_ref, *rest, page, tq, n_step_pages, lam_init):
    del pt_ref
    pages = rest[:4 * n_step_pages]
    osb_ref, oda_ref = rest[4 * n_step_pages:4 * n_step_pages + 2]
    qbd_s, q2_s, sb_acc, sb_run, da_m, da_l, da_acc = rest[4 * n_step_pages + 2:]
    step = pl.program_id(1)
    last = pl.num_programs(1) - 1
    n_sb = SB_HEADS * tq
    n_da = 2 * DA_HEADS * tq
    upper = _suffix_matrix(page)

    def iota(shape, axis):
        return lax.broadcasted_iota(jnp.int32, shape, axis)

    da_head_of_row = iota((n_da, DA_HEADS * page), 0) // (2 * tq)
    da_page_valid = (iota((n_da, DA_HEADS * page), 1) % DA_HEADS) == da_head_of_row

    def pad_rows(x):
        return jnp.concatenate([x, jnp.zeros((page - x.shape[0], x.shape[1]), x.dtype)], axis=0)

    @pl.when(step == 0)
    def _():
        q = sbq_ref[0].astype(F32)
        q_rows = jnp.concatenate([q] * SB_HEADS, axis=0)
        own = (iota((n_sb, GROUP_WIDTH), 1) // HEAD_DIM) == (iota((n_sb, GROUP_WIDTH), 0) // tq)
        qbd = jnp.where(own, q_rows, 0.0).astype(BF16)
        qbd_s[...] = qbd
        q = daq_ref[0].astype(F32)
        lane = iota((tq, LANES), 1)
        pieces = []
        for h in range(DA_HEADS):
            slab = q[:, h * LANES:(h + 1) * LANES]
            pieces += [jnp.where(lane < HEAD_DIM, slab, 0.0), jnp.where(lane < HEAD_DIM, 0.0, slab)]
        q2 = jnp.concatenate(pieces, axis=0).astype(BF16)
        q2_s[...] = q2

        t_sb = iota((n_sb, page), 0) % tq
        a, run = _sb_weights(_nt_dot(qbd, pad_rows(sbkn_ref[0])), jnp.zeros((n_sb, 1), F32),
                             iota((n_sb, page), 1) < t_sb, upper)
        sb_acc[...] = jnp.dot(a, pad_rows(sbvn_ref[0]), preferred_element_type=F32)
        sb_run[...] = run

        c = iota((n_da, page), 1)
        r = iota((n_da, page), 0)
        valid = jnp.logical_and((c % DA_HEADS) == r // (2 * tq), c // DA_HEADS <= r % tq)
        m, _, (p,), l = _da_weights([_nt_dot(q2, pad_rows(dakn_ref[0]))], jnp.full((n_da, LANES), -jnp.inf, F32),
                                    jnp.zeros((n_da, LANES), F32), valid)
        da_m[...] = m
        da_l[...] = l
        da_acc[...] = jnp.dot(p, pad_rows(davn_ref[0]), preferred_element_type=F32)

    def sb_groups(g0):
        if g0 >= n_step_pages:
            return

        @pl.when(jnp.min(sb_run[...]) < EXP_ZERO_ABOVE)
        def _():
            run = sb_run[...]
            acc = sb_acc[...]
            for i in range(g0, min(g0 + SB_PAGE_GROUP, n_step_pages)):
                kt = pages[4 * i][0, 0].astype(BF16)
                vt = pages[4 * i + 1][0, 0].astype(BF16)
                a, run = _sb_weights(jnp.dot(qbd_s[...], kt, preferred_element_type=F32), run, None, upper)
                acc = acc + _nt_dot(a, vt)
            sb_acc[...] = acc
            sb_run[...] = run
            sb_groups(g0 + SB_PAGE_GROUP)

    sb_groups(0)

    tiles = [_nt_dot(q2_s[...], pages[4 * i + 2][0, 0].astype(BF16))
             for i in range(n_step_pages)]
    m, alpha, ps, l = _da_weights(tiles, da_m[...], da_l[...], da_page_valid)
    acc = alpha * da_acc[...]
    for i, p in enumerate(ps):
        acc = acc + jnp.dot(p, pages[4 * i + 3][0, 0].astype(BF16), preferred_element_type=F32)
    da_m[...] = m
    da_l[...] = l
    da_acc[...] = acc

    @pl.when(step == last)
    def _():
        acc = sb_acc[...]
        col_head = iota((tq, GROUP_WIDTH), 1) // HEAD_DIM
        o = jnp.zeros((tq, GROUP_WIDTH), F32)
        for h in range(SB_HEADS):
            o = o + jnp.where(col_head == h, acc[h * tq:(h + 1) * tq], 0.0)
        osb_ref[0] = (o * sbg_ref[0]).astype(BF16)

        lam = _lambda(lam_ref, lam_init)
        acc, l = da_acc[...], da_l[...]
        for h in range(DA_HEADS):
            r1, r2 = 2 * h * tq, (2 * h + 1) * tq
            oda_ref[0, :, h * LANES:(h + 1) * LANES] = _da_finish(
                acc[r1:r1 + tq], l[r1:r1 + tq], acc[r2:r2 + tq], l[r2:r2 + tq], lam, subln_ref[...],
                lam_init, dag_ref[0, :, h * LANES:(h + 1) * LANES])


def _sample_attention(page_table, new, caches, layer, lam_vecs, subln_g, *, lam_init):
    sbq = new[0]
    db, tq, _ = sbq.shape
    n_pages = page_table.shape[1]
    page = caches[0].shape[3]
    n_step_pages = math.gcd(PAGES_PER_STEP, n_pages)

    def new_spec(a):
        return pl.BlockSpec((1,) + a.shape[1:], lambda b, st, pt: (b, 0, 0))

    def page_spec(i):
        def index(b, st, pt):
            return (layer, pt[b, n_pages - 1 - (st * n_step_pages + i)], 0, 0)
        return pl.BlockSpec((1, 1, GROUP_WIDTH, page), index)

    small = lambda e: pl.BlockSpec(e.shape, lambda b, st, pt: (0, 0))
    page_specs, page_args = [], []
    for i in range(n_step_pages):
        page_specs += [page_spec(i)] * 4
        page_args += list(caches)
    n_sb, n_da = SB_HEADS * tq, 2 * DA_HEADS * tq
    out = jax.ShapeDtypeStruct((db, tq, GROUP_WIDTH), BF16)
    out_spec = pl.BlockSpec((1, tq, GROUP_WIDTH), lambda b, st, pt: (b, 0, 0))
    grid_spec = pltpu.PrefetchScalarGridSpec(
        num_scalar_prefetch=1,
        grid=(db, n_pages // n_step_pages),
        in_specs=[new_spec(a) for a in new] + [small(lam_vecs), small(subln_g)] + page_specs,
        out_specs=[out_spec, out_spec],
        scratch_shapes=[
            pltpu.VMEM((n_sb, GROUP_WIDTH), BF16), pltpu.VMEM((n_da, LANES), BF16),
            pltpu.VMEM((n_sb, GROUP_WIDTH), F32), pltpu.VMEM((n_sb, 1), F32),
            pltpu.VMEM((n_da, LANES), F32), pltpu.VMEM((n_da, LANES), F32), pltpu.VMEM((n_da, LANES), F32),
        ],
    )
    return pl.pallas_call(
        functools.partial(_sample_kernel, page=page, tq=tq, n_step_pages=n_step_pages, lam_init=lam_init),
        grid_spec=grid_spec,
        out_shape=[out, out],
        compiler_params=_cparams(("arbitrary", "arbitrary")),
        name="sample_attention",
    )(page_table, *new, lam_vecs, subln_g, *page_args)


def _outproj_kernel(x_ref, msb_ref, mda_ref, w_ref, gate_ref, fg_ref, o_ref, *, final):
    half = msb_ref.shape[1]
    y = (jnp.dot(msb_ref[...], w_ref[:half], preferred_element_type=F32)
         + jnp.dot(mda_ref[...], w_ref[half:], preferred_element_type=F32))
    y = x_ref[...] + gate_ref[...] * y
    if final:
        y = y * lax.rsqrt(jnp.mean(y * y, axis=-1, keepdims=True) + EPS) * fg_ref[...]
    o_ref[...] = y


def _outproj(x, m_sb, m_da, w16, gate, final_g, *, tm, rows_per_group, final):
    rows, d = x.shape
    if rows_per_group is None:
        gate_spec = pl.BlockSpec((tm, d), lambda i: (i, 0))
    else:
        per = rows_per_group // tm
        gate_spec = pl.BlockSpec((None, 1, d), lambda i: (i // per, 0, 0))
    row_spec = pl.BlockSpec((tm, d), lambda i: (i, 0))
    mix_spec = pl.BlockSpec((tm, GROUP_WIDTH), lambda i: (i, 0))
    return pl.pallas_call(
        functools.partial(_outproj_kernel, final=final),
        grid=(rows // tm,),
        in_specs=[row_spec, mix_spec, mix_spec, pl.BlockSpec(w16.shape, lambda i: (0, 0)),
                  gate_spec, pl.BlockSpec((1, d), lambda i: (0, 0))],
        out_specs=row_spec,
        out_shape=jax.ShapeDtypeStruct((rows, d), F32),
        compiler_params=_cparams(("arbitrary",)),
        name="outproj",
    )(x, m_sb, m_da, w16, gate, final_g)


def _rope_tables(pos):
    inv_freq = ROPE_THETA ** (-jnp.arange(0, HEAD_DIM, 2, dtype=F32) / HEAD_DIM)
    ang = pos.astype(F32)[:, None] * inv_freq[None, :]
    ang = jnp.concatenate([ang, ang, ang, ang], axis=-1)
    first_half = (jnp.arange(LANES) % HEAD_DIM) < HEAD_DIM // 2
    sin = jnp.sin(ang)
    return jnp.cos(ang), jnp.where(first_half, -sin, 0.0), jnp.where(first_half, 0.0, sin)


def _pick_tile(rows, target):
    tm = min(rows, target)
    while rows % tm:
        tm //= 2
    return tm


def kernel(x_prompt, x_sample, c_prompt, c_sample, cache_sb_k, cache_sb_v, cache_da_k, cache_da_v,
           page_table, w_mod, b_mod, norm_g, w_in, w_out, lambda_q1, lambda_k1, lambda_q2, lambda_k2,
           subln_g, final_norm_g):
    bsz, seq, d = x_prompt.shape
    db, dseq, _ = x_sample.shape
    depth = w_in.shape[0]
    n_pool, page = cache_sb_k.shape[1], cache_sb_k.shape[2]
    n_pages = page_table.shape[1]
    past_len = n_pages * page

    n_c = bsz + db
    c_all = jnp.concatenate([c_prompt, c_sample, jnp.zeros((-n_c % 8, d), F32)], axis=0)
    mod = _modulation(c_all, w_mod, b_mod)

    tabs_p = _rope_tables(jnp.arange(seq, dtype=jnp.int32))
    tabs_s = tuple(jnp.tile(t, (db, 1))
                   for t in _rope_tables(past_len + jnp.arange(dseq, dtype=jnp.int32)))

    caches = tuple(jnp.transpose(c, (0, 1, 3, 4, 2)).reshape(depth, n_pool, GROUP_WIDTH, page)
                   for c in (cache_sb_k, cache_sb_v))
    caches += tuple(c.reshape(depth, n_pool, page * DA_HEADS, 2 * HEAD_DIM)
                    for c in (cache_da_k, cache_da_v))
    w_in16 = w_in.astype(BF16)
    w_out16 = w_out.astype(BF16)
    lam_all = jnp.stack([lambda_q1, lambda_k1, lambda_q2, lambda_k2], axis=1)
    fg = final_norm_g.reshape(1, d)

    rows_p, rows_s = bsz * seq, db * dseq
    tm_p = _pick_tile(seq, 256)
    tm_out = _pick_tile(seq, 512)
    tm_s = _pick_tile(rows_s, 256)
    sb_blk = _pick_tile(seq, SB_BLOCK)
    da_blk = _pick_tile(seq, DA_BLOCK)

    hp = x_prompt.reshape(rows_p, d)
    hs = x_sample.reshape(rows_s, d)
    new_p, new_s = [], []
    for l in range(depth):
        lam_init = 0.8 - 0.6 * math.exp(-0.3 * l)
        g_l = norm_g[l].reshape(1, d)
        subln = subln_g[l].reshape(1, LANES)
        final = l == depth - 1

        def per_row(v):
            return jnp.broadcast_to(v[:, None, :], (db, dseq, d)).reshape(rows_s, d)

        shift_p, scale_p, gate_p = (mod[l, j, :bsz].reshape(bsz, 1, d) for j in range(3))
        shift_s, scale_s, gate_s = (per_row(mod[l, j, bsz:n_c]) for j in range(3))

        (sbq, sbk, sbv, sbg, daq, dak, dav, dag, sbk16, sbv16, dak16, dav16) = _inproj(
            hp, scale_p, shift_p, g_l, w_in16[l], *tabs_p, tm=tm_p, rows_per_group=seq)
        new_p.append((sbk, sbv, dak, dav))
        r3 = lambda a: a.reshape(bsz, seq, GROUP_WIDTH)
        m_sb = _prompt_attention(functools.partial(_sb_prompt_kernel, blk=sb_blk), "sb_prompt",
                                 r3(sbq), r3(sbk16), r3(sbv16), r3(sbg), (), [], blk=sb_blk)
        da_scratch = [pltpu.VMEM((2 * da_blk, da_blk), F32), pltpu.VMEM((2 * da_blk, da_blk), F32),
                      pltpu.VMEM((2 * da_blk, LANES), F32), pltpu.VMEM((2 * da_blk, LANES), F32),
                      pltpu.VMEM((2 * da_blk, LANES), F32)]
        m_da = _prompt_attention(functools.partial(_da_prompt_kernel, blk=da_blk, lam_init=lam_init),
                                 "da_prompt", r3(daq), r3(dak16), r3(dav16), r3(dag),
                                 (lam_all[l], subln), da_scratch, blk=da_blk)
        hp = _outproj(hp, m_sb.reshape(rows_p, GROUP_WIDTH), m_da.reshape(rows_p, GROUP_WIDTH),
                      w_out16[l], gate_p, fg, tm=tm_out, rows_per_group=seq, final=final)

        (sbq, sbk, sbv, sbg, daq, dak, dav, dag, sbk16, sbv16, dak16, dav16) = _inproj(
            hs, scale_s, shift_s, g_l, w_in16[l], *tabs_s, tm=tm_s, rows_per_group=None)
        new_s.append((sbk, sbv, dak, dav))
        s3 = lambda a: a.reshape(db, dseq, GROUP_WIDTH)
        da_rows = lambda a: a.reshape(db, dseq * DA_HEADS, 2 * HEAD_DIM)
        m_sb, m_da = _sample_attention(
            page_table, (s3(sbq), s3(sbk16), s3(sbv16), s3(sbg), s3(daq), da_rows(dak16), da_rows(dav16),
                         s3(dag)),
            caches, l, lam_all[l], subln, lam_init=lam_init)
        hs = _outproj(hs, m_sb.reshape(rows_s, GROUP_WIDTH), m_da.reshape(rows_s, GROUP_WIDTH),
                      w_out16[l], gate_s, fg, tm=tm_s, rows_per_group=None, final=final)

    def stacked(rows, idx):
        return jnp.stack([r[idx] for r in rows], axis=0)

    sb_heads = (SB_HEADS, HEAD_DIM)
    da_heads = (DA_HEADS, 2 * HEAD_DIM)
    outs_p = [jnp.transpose(stacked(new_p, i).reshape((depth, bsz) + sb_heads + (seq,)), (0, 1, 4, 2, 3))
              for i in range(2)]
    outs_p += [stacked(new_p, i).reshape((depth, bsz, seq) + da_heads) for i in range(2, 4)]
    outs_s = [stacked(new_s, i).reshape((depth, db, dseq) + (sb_heads if i < 2 else da_heads))
              for i in range(4)]
    return (hp.reshape(bsz, seq, d), hs.reshape(db, dseq, d), *outs_p, *outs_s)
```

```python
import functools
import math

import jax
import jax.numpy as jnp
from jax import lax
from jax.experimental import pallas as pl
from jax.experimental.pallas import tpu as pltpu

F32 = jnp.float32
BF16 = jnp.bfloat16

HEAD_DIM = 64
LANES = 128
SLABS = 4
GROUP_WIDTH = SLABS * LANES
SB_HEADS = GROUP_WIDTH // HEAD_DIM
DA_HEADS = GROUP_WIDTH // (2 * HEAD_DIM)
ROPE_THETA = 10000.0
EPS = 1e-6
Q_SCALE = HEAD_DIM ** -0.5
LOG2E = math.log2(math.e)
EXP_ZERO_ABOVE = 104.0
VMEM_LIMIT = 48 * 1024 * 1024
SB_BLOCK = 256
DA_BLOCK = 512
PAGES_PER_STEP = 16
SB_PAGE_GROUP = 2
SB_FAST_PAGES = 4


def _cparams(sem):
    return pltpu.CompilerParams(dimension_semantics=sem, vmem_limit_bytes=VMEM_LIMIT)


def _mod_kernel(c_ref, w_ref, b_ref, o_ref):
    c = c_ref[...]
    sc = c / (1.0 + jnp.exp(-c))
    o_ref[0, 0] = jnp.dot(sc, w_ref[0], precision=lax.Precision.HIGHEST,
                          preferred_element_type=F32) + b_ref[0, 0]


def _modulation(c_all, w_mod, b_mod):
    depth, d, _ = w_mod.shape
    r = c_all.shape[0]
    return pl.pallas_call(
        _mod_kernel,
        grid=(depth, 3),
        in_specs=[
            pl.BlockSpec((r, d), lambda l, j: (0, 0)),
            pl.BlockSpec((1, d, d), lambda l, j: (l, 0, j)),
            pl.BlockSpec((1, 1, 1, d), lambda l, j: (l, j, 0, 0)),
        ],
        out_specs=pl.BlockSpec((1, 1, r, d), lambda l, j: (l, j, 0, 0)),
        out_shape=jax.ShapeDtypeStruct((depth, 3, r, d), F32),
        compiler_params=_cparams(("arbitrary", "arbitrary")),
        name="modulation",
    )(c_all, w_mod, b_mod.reshape(depth, 3, 1, d))


def _inproj_kernel(x_ref, scale_ref, shift_ref, g_ref, w_ref, cos_ref, sin_up_ref, sin_dn_ref,
                   sbq_ref, sbk_ref, sbv_ref, sbg_ref, daq_ref, dak_ref, dav_ref, dag_ref,
                   sbk16_ref, sbv16_ref, dak16_ref, dav16_ref, *, sb_transposed):
    x = x_ref[...]
    tm = x.shape[0]
    xn = x * lax.rsqrt(jnp.mean(x * x, axis=-1, keepdims=True) + EPS)
    h = (xn * g_ref[...]) * (1.0 + scale_ref[...]) + shift_ref[...]
    h16 = h.astype(BF16)

    def seg(i):
        return jnp.dot(h16, w_ref[:, i * GROUP_WIDTH:(i + 1) * GROUP_WIDTH],
                       preferred_element_type=F32)

    def rope_slab(p):
        return (p * cos_ref[...] + pltpu.roll(p, LANES - HEAD_DIM // 2, 1) * sin_up_ref[...]
                + pltpu.roll(p, HEAD_DIM // 2, 1) * sin_dn_ref[...])

    sbq_ref[...] = (seg(0) * Q_SCALE).astype(BF16)
    k = seg(1)
    if sb_transposed:
        sbk_ref[0] = k.T
    else:
        sbk_ref[...] = k
    sbk16_ref[...] = k.astype(BF16)
    v = seg(2)
    if sb_transposed:
        sbv_ref[0] = v.T
    else:
        sbv_ref[...] = v
    sbv16_ref[...] = v.astype(BF16)
    g = seg(3)
    sbg_ref[...] = g / (1.0 + jnp.exp(-g))
    q = seg(4)
    for s in range(SLABS):
        sl = slice(s * LANES, (s + 1) * LANES)
        daq_ref[:, sl] = (rope_slab(q[:, sl]) * (Q_SCALE * LOG2E)).astype(BF16)
    k = seg(5)
    for s in range(SLABS):
        sl = slice(s * LANES, (s + 1) * LANES)
        kr = rope_slab(k[:, sl])
        dak_ref[pl.ds(s, tm, stride=SLABS), :] = kr
        dak16_ref[:, sl] = kr.astype(BF16)
    v = seg(6)
    for s in range(SLABS):
        dav_ref[pl.ds(s, tm, stride=SLABS), :] = v[:, s * LANES:(s + 1) * LANES]
    dav16_ref[...] = v.astype(BF16)
    g = seg(7)
    dag_ref[...] = g / (1.0 + jnp.exp(-g))


def _inproj(x, scale, shift, norm_g, w16, cos_t, sin_up, sin_dn, *, tm, rows_per_group):
    rows, d = x.shape
    nblk = rows // tm
    if rows_per_group is None:
        mod_spec = pl.BlockSpec((tm, d), lambda i: (i, 0))
    else:
        per = rows_per_group // tm
        mod_spec = pl.BlockSpec((None, 1, d), lambda i: (i // per, 0, 0))
    tab_blocks = cos_t.shape[0] // tm
    tab_spec = pl.BlockSpec((tm, LANES), lambda i: (i % tab_blocks, 0))
    out_spec = pl.BlockSpec((tm, GROUP_WIDTH), lambda i: (i, 0))
    f32o = jax.ShapeDtypeStruct((rows, GROUP_WIDTH), F32)
    b16o = jax.ShapeDtypeStruct((rows, GROUP_WIDTH), BF16)
    da_spec = pl.BlockSpec((tm * SLABS, LANES), lambda i: (i, 0))
    da_o = jax.ShapeDtypeStruct((rows * SLABS, LANES), F32)
    if rows_per_group is None:
        sb_spec, sb_o = out_spec, f32o
    else:
        sb_spec = pl.BlockSpec((1, GROUP_WIDTH, tm), lambda i: (i // per, 0, i % per))
        sb_o = jax.ShapeDtypeStruct((rows // rows_per_group, GROUP_WIDTH, rows_per_group), F32)
    return pl.pallas_call(
        functools.partial(_inproj_kernel, sb_transposed=rows_per_group is not None),
        grid=(nblk,),
        in_specs=[
            pl.BlockSpec((tm, d), lambda i: (i, 0)),
            mod_spec, mod_spec,
            pl.BlockSpec((1, d), lambda i: (0, 0)),
            pl.BlockSpec(w16.shape, lambda i: (0, 0)),
            tab_spec, tab_spec, tab_spec,
        ],
        out_specs=[out_spec, sb_spec, sb_spec, out_spec, out_spec, da_spec, da_spec, out_spec] + [out_spec] * 4,
        out_shape=[b16o, sb_o, sb_o, f32o, b16o, da_o, da_o, f32o, b16o, b16o, b16o, b16o],
        compiler_params=_cparams(("arbitrary",)),
        name="inproj",
    )(x, scale, shift, norm_g, w16, cos_t, sin_up, sin_dn)


def _nt_dot(a, b):
    return lax.dot_general(a, b, (((1,), (1,)), ((), ())), preferred_element_type=F32)


def _suffix_matrix(tk):
    j = lax.broadcasted_iota(jnp.int32, (tk, tk), 0)
    s = lax.broadcasted_iota(jnp.int32, (tk, tk), 1)
    return jnp.where(j > s, 1.0, 0.0).astype(BF16)


def _sb_weights(z, run, valid, upper):
    e = jnp.log(1.0 + jnp.exp(-jnp.abs(z)))
    sp = jnp.maximum(z, 0.0) + e
    ls = jnp.minimum(z, 0.0) - e
    if valid is not None:
        sp = jnp.where(valid, sp, 0.0)
    sp_hi = sp.astype(BF16)
    sp_lo = (sp - sp_hi.astype(F32)).astype(BF16)
    after = (jnp.dot(sp_hi, upper, preferred_element_type=F32)
             + jnp.dot(sp_lo, upper, preferred_element_type=F32))
    a = jnp.exp(ls - (after + run))
    if valid is not None:
        a = jnp.where(valid, a, 0.0)
    return a.astype(BF16), run + jnp.sum(sp, axis=-1, keepdims=True)


def _lane_chunks(x):
    return [x[:, c * LANES:(c + 1) * LANES] for c in range(x.shape[1] // LANES)]


def _da_weights(tiles, m, l, valid):
    if valid is not None:
        tiles = [jnp.where(valid, s, -jnp.inf) for s in tiles]
    chunks = [_lane_chunks(s) for s in tiles]
    top = chunks[0][0]
    for c in [c for cs in chunks for c in cs][1:]:
        top = jnp.maximum(top, c)
    m_new = jnp.maximum(m, jnp.broadcast_to(jnp.max(top, axis=-1, keepdims=True), m.shape))
    alpha = jnp.exp2(m - m_new)
    l_new = alpha * l
    ps = []
    for cs in chunks:
        pcs = [jnp.exp2(c - m_new) for c in cs]
        for pc in pcs:
            l_new = l_new + pc
        ps.append(jnp.concatenate([pc.astype(BF16) for pc in pcs], axis=1))
    return m_new, alpha, ps, l_new


def _lambda(lam_ref, lam_init):
    lam = lam_ref[...]
    s1 = jnp.sum(lam[0:1] * lam[1:2], axis=-1, keepdims=True)
    s2 = jnp.sum(lam[2:3] * lam[3:4], axis=-1, keepdims=True)
    return jnp.exp(s1) - jnp.exp(s2) + lam_init


def _da_finish(acc1, l1, acc2, l2, lam, subln_g, lam_init, gate):
    o = (acc1 / jnp.sum(l1, axis=-1, keepdims=True)
         - lam * (acc2 / jnp.sum(l2, axis=-1, keepdims=True)))
    on = o * lax.rsqrt(jnp.mean(o * o, axis=-1, keepdims=True) + EPS)
    return ((on * subln_g) * (1.0 - lam_init) * gate).astype(BF16)


def _stack_queries(q):
    lane = lax.broadcasted_iota(jnp.int32, q.shape, 1)
    zero = jnp.zeros_like(q)
    return jnp.concatenate([jnp.where(lane < HEAD_DIM, q, zero),
                            jnp.where(lane < HEAD_DIM, zero, q)], axis=0)


def _row_col(m, tk, tq):
    row = lax.broadcasted_iota(jnp.int32, (m, tk), 0)
    col = lax.broadcasted_iota(jnp.int32, (m, tk), 1)
    return jnp.where(row >= tq, row - tq, row), col


def _sb_prompt_kernel(q_ref, k_ref, v_ref, g_ref, o_ref, *, blk):
    i = pl.program_id(2)
    q2 = _stack_queries(q_ref[0])
    upper = _suffix_matrix(blk)
    t, s = _row_col(2 * blk, blk, blk)

    def kv(j):
        start = pl.multiple_of(j * blk, blk)
        return k_ref[0, pl.ds(start, blk), :], v_ref[0, pl.ds(start, blk), :]

    def block(j, run, valid):
        k, v = kv(j)
        a, run = _sb_weights(_nt_dot(q2, k), run, valid, upper)
        return jnp.dot(a, v, preferred_element_type=F32), run

    def pair(j, run, valid_hi, valid_lo):
        pv_hi, run = block(j, run, valid_hi)
        pv_lo, run = block(jnp.maximum(j - 1, 0), run, valid_lo)
        return pv_hi + pv_lo, run

    def live(run):
        return jnp.min(run) < EXP_ZERO_ABOVE

    acc, run = pair(i, jnp.zeros((2 * blk, 1), F32), s < t, s >= jnp.where(i >= 1, 0, blk))

    def cond(c):
        j, _, run = c
        return jnp.logical_and(j >= 1, live(run))

    def body(c):
        j, acc, run = c
        pv, run = pair(j, run, None, None)
        return j - 2, acc + pv, run

    j, acc, run = lax.while_loop(cond, body, (i - 2, acc, run))

    def last_block(acc, run):
        pv, run = block(0, run, None)
        return acc + pv, run

    acc, _ = lax.cond(jnp.logical_and(j == 0, live(run)), last_block, lambda acc, run: (acc, run), acc, run)
    lane = lax.broadcasted_iota(jnp.int32, (blk, LANES), 1)
    o = jnp.where(lane < HEAD_DIM, acc[:blk], acc[blk:])
    o_ref[0] = (o * g_ref[0]).astype(BF16)


def _da_prompt_kernel(q_ref, k_ref, v_ref, g_ref, lam_ref, subln_ref, o_ref,
                      s_even, s_odd, m_s, l_s, acc_s, *, blk, lam_init):
    i = pl.program_id(2)
    q2 = _stack_queries(q_ref[0])
    m2 = 2 * blk

    def scores_into(s_ref, j):
        start = pl.multiple_of(j * blk, blk)
        s_ref[...] = _nt_dot(q2, k_ref[0, pl.ds(start, blk), :])

    def absorb(s_ref, j, valid):
        start = pl.multiple_of(j * blk, blk)
        m, alpha, (p,), l = _da_weights([s_ref[...]], m_s[...], l_s[...], valid)
        m_s[...] = m
        l_s[...] = l
        acc_s[...] = alpha * acc_s[...] + jnp.dot(p, v_ref[0, pl.ds(start, blk), :],
                                                  preferred_element_type=F32)

    m_s[...] = jnp.full((m2, LANES), -jnp.inf, F32)
    l_s[...] = jnp.zeros((m2, LANES), F32)
    acc_s[...] = jnp.zeros((m2, LANES), F32)
    scores_into(s_even, 0)

    def pair(jj, carry):
        j = 2 * jj
        scores_into(s_odd, j + 1)
        absorb(s_even, j, None)
        scores_into(s_even, j + 2)
        absorb(s_odd, j + 1, None)
        return carry

    lax.fori_loop(0, i // 2, pair, 0)
    t, s = _row_col(m2, blk, blk)
    diag = s <= t

    @pl.when(i % 2 == 0)
    def _():
        absorb(s_even, i, diag)

    @pl.when(i % 2 == 1)
    def _():
        scores_into(s_odd, i)
        absorb(s_even, i - 1, None)
        absorb(s_odd, i, diag)

    acc, l = acc_s[...], l_s[...]
    o_ref[0] = _da_finish(acc[:blk], l[:blk], acc[blk:], l[blk:], _lambda(lam_ref, lam_init),
                          subln_ref[...], lam_init, g_ref[0])


def _prompt_attention(kernel, name, q, k, v, g, extra, scratch, *, blk):
    b, t, _ = q.shape
    qspec = pl.BlockSpec((1, blk, LANES), lambda bi, s, i: (bi, i, s))
    kvspec = pl.BlockSpec((1, t, LANES), lambda bi, s, i: (bi, 0, s))
    extra_specs = [pl.BlockSpec(e.shape, lambda bi, s, i: (0, 0)) for e in extra]
    return pl.pallas_call(
        kernel,
        grid=(b, SLABS, t // blk),
        in_specs=[qspec, kvspec, kvspec, qspec] + extra_specs,
        out_specs=qspec,
        out_shape=jax.ShapeDtypeStruct((b, t, GROUP_WIDTH), BF16),
        scratch_shapes=scratch,
        compiler_params=_cparams(("arbitrary", "arbitrary", "arbitrary")),
        name=name,
    )(q, k, v, g, *extra)


def _block_diag_queries(q, tq):
    n_sb = SB_HEADS * tq
    q_rows = jnp.concatenate([q] * SB_HEADS, axis=0)
    own = (lax.broadcasted_iota(jnp.int32, (n_sb, GROUP_WIDTH), 1) // HEAD_DIM
           == lax.broadcasted_iota(jnp.int32, (n_sb, GROUP_WIDTH), 0) // tq)
    return jnp.where(own, q_rows, 0.0).astype(BF16)


def _sb_page_groups(kv_refs, qbd_ref, acc_ref, run_ref, upper):
    def groups(g0):
        if g0 >= len(kv_refs):
            return

        @pl.when(jnp.min(run_ref[...]) < EXP_ZERO_ABOVE)
        def _():
            run = run_ref[...]
            acc = acc_ref[...]
            for k_ref, v_ref in kv_refs[g0:g0 + SB_PAGE_GROUP]:
                kt = k_ref[0, 0].astype(BF16)
                vt = v_ref[0, 0].astype(BF16)
                a, run = _sb_weights(jnp.dot(qbd_ref[...], kt, preferred_element_type=F32), run, None, upper)
                acc = acc + _nt_dot(a, vt)
            acc_ref[...] = acc
            run_ref[...] = run
            groups(g0 + SB_PAGE_GROUP)

    groups(0)


def _sb_sample_output(acc, gate, tq):
    col_head = lax.broadcasted_iota(jnp.int32, (tq, GROUP_WIDTH), 1) // HEAD_DIM
    o = jnp.zeros((tq, GROUP_WIDTH), F32)
    for h in range(SB_HEADS):
        o = o + jnp.where(col_head == h, acc[h * tq:(h + 1) * tq], 0.0)
    return (o * gate).astype(BF16)


def _sample_kernel(pt_ref, sbq_ref, sbkn_ref, sbvn_ref, daq_ref, dakn_ref, davn_ref, dag_ref,
                   lam_ref, subln_ref, *rest, page, tq, n_step_pages, n_sb_pages, lam_init):
    del pt_ref
    sb_pages = [(rest[2 * i], rest[2 * i + 1]) for i in range(n_sb_pages)]
    rest = rest[2 * n_sb_pages:]
    da_pages = [(rest[2 * i], rest[2 * i + 1]) for i in range(n_step_pages)]
    osb_acc_ref, osb_run_ref, oda_ref = rest[2 * n_step_pages:2 * n_step_pages + 3]
    qbd_s, q2_s, sb_acc, sb_run, da_m, da_l, da_acc = rest[2 * n_step_pages + 3:]
    step = pl.program_id(1)
    last = pl.num_programs(1) - 1
    n_sb = SB_HEADS * tq
    n_da = 2 * DA_HEADS * tq
    upper = _suffix_matrix(page)

    def iota(shape, axis):
        return lax.broadcasted_iota(jnp.int32, shape, axis)

    da_head_of_row = iota((n_da, DA_HEADS * page), 0) // (2 * tq)
    da_page_valid = (iota((n_da, DA_HEADS * page), 1) % DA_HEADS) == da_head_of_row

    def pad_rows(x):
        return jnp.concatenate([x, jnp.zeros((page - x.shape[0], x.shape[1]), x.dtype)], axis=0)

    @pl.when(step == 0)
    def _():
        qbd = _block_diag_queries(sbq_ref[0].astype(F32), tq)
        qbd_s[...] = qbd
        q = daq_ref[0].astype(F32)
        lane = iota((tq, LANES), 1)
        pieces = []
        for h in range(DA_HEADS):
            slab = q[:, h * LANES:(h + 1) * LANES]
            pieces += [jnp.where(lane < HEAD_DIM, slab, 0.0), jnp.where(lane < HEAD_DIM, 0.0, slab)]
        q2 = jnp.concatenate(pieces, axis=0).astype(BF16)
        q2_s[...] = q2

        t_sb = iota((n_sb, page), 0) % tq
        a, run = _sb_weights(_nt_dot(qbd, pad_rows(sbkn_ref[0])), jnp.zeros((n_sb, 1), F32),
                             iota((n_sb, page), 1) < t_sb, upper)
        sb_acc[...] = jnp.dot(a, pad_rows(sbvn_ref[0]), preferred_element_type=F32)
        sb_run[...] = run

        c = iota((n_da, page), 1)
        r = iota((n_da, page), 0)
        valid = jnp.logical_and((c % DA_HEADS) == r // (2 * tq), c // DA_HEADS <= r % tq)
        m, _, (p,), l = _da_weights([_nt_dot(q2, pad_rows(dakn_ref[0]))], jnp.full((n_da, LANES), -jnp.inf, F32),
                                    jnp.zeros((n_da, LANES), F32), valid)
        da_m[...] = m
        da_l[...] = l
        da_acc[...] = jnp.dot(p, pad_rows(davn_ref[0]), preferred_element_type=F32)

    @pl.when(step == 0)
    def _():
        _sb_page_groups(sb_pages, qbd_s, sb_acc, sb_run, upper)

    tiles = [_nt_dot(q2_s[...], k_ref[0, 0].astype(BF16))
             for k_ref, _ in da_pages]
    m, alpha, ps, l = _da_weights(tiles, da_m[...], da_l[...], da_page_valid)
    acc = alpha * da_acc[...]
    for (_, v_ref), p in zip(da_pages, ps):
        acc = acc + jnp.dot(p, v_ref[0, 0].astype(BF16), preferred_element_type=F32)
    da_m[...] = m
    da_l[...] = l
    da_acc[...] = acc

    @pl.when(step == last)
    def _():
        osb_acc_ref[0] = sb_acc[...]
        osb_run_ref[0] = jnp.broadcast_to(sb_run[...], (n_sb, LANES))

        lam = _lambda(lam_ref, lam_init)
        acc, l = da_acc[...], da_l[...]
        for h in range(DA_HEADS):
            r1, r2 = 2 * h * tq, (2 * h + 1) * tq
            oda_ref[0, :, h * LANES:(h + 1) * LANES] = _da_finish(
                acc[r1:r1 + tq], l[r1:r1 + tq], acc[r2:r2 + tq], l[r2:r2 + tq], lam, subln_ref[...],
                lam_init, dag_ref[0, :, h * LANES:(h + 1) * LANES])


def _sb_tail_kernel(pt_ref, dead_ref, acc_ref, run_ref, sbq_ref, sbg_ref, *rest, page, tq):
    del pt_ref
    n_tail = (len(rest) - 4) // 2
    kv_refs = [(rest[2 * i], rest[2 * i + 1]) for i in range(n_tail)]
    osb_ref, qbd_s, acc_s, run_s = rest[2 * n_tail:]
    acc_s[...] = acc_ref[0]
    run_s[...] = run_ref[0][:, :1]

    @pl.when(dead_ref[pl.program_id(0)] == 0)
    def _():
        qbd_s[...] = _block_diag_queries(sbq_ref[0].astype(F32), tq)
        _sb_page_groups(kv_refs, qbd_s, acc_s, run_s, _suffix_matrix(page))

    osb_ref[0] = _sb_sample_output(acc_s[...], sbg_ref[0], tq)


def _sample_attention(page_table, new, sbg, caches, layer, lam_vecs, subln_g, *, lam_init):
    sbq = new[0]
    db, tq, _ = sbq.shape
    n_pages = page_table.shape[1]
    page = caches[0].shape[3]
    n_step_pages = math.gcd(PAGES_PER_STEP, n_pages)
    n_fast = min(SB_FAST_PAGES, n_step_pages)
    n_sb, n_da = SB_HEADS * tq, 2 * DA_HEADS * tq
    page_block = (1, 1, GROUP_WIDTH, page)

    def new_spec(a):
        return pl.BlockSpec((1,) + a.shape[1:], lambda b, *_: (b, 0, 0))

    def small(e):
        return pl.BlockSpec(e.shape, lambda *_: (0, 0))

    def sb_fast_spec(i):
        return pl.BlockSpec(page_block, lambda b, st, pt: (layer, pt[b, n_pages - 1 - i], 0, 0))

    def da_spec(i):
        return pl.BlockSpec(page_block,
                            lambda b, st, pt: (layer, pt[b, n_pages - 1 - (st * n_step_pages + i)], 0, 0))

    page_specs = [sb_fast_spec(i) for i in range(n_fast) for _ in range(2)]
    page_args = [c for _ in range(n_fast) for c in caches[:2]]
    page_specs += [da_spec(i) for i in range(n_step_pages) for _ in range(2)]
    page_args += [c for _ in range(n_step_pages) for c in caches[2:]]
    acc_spec = pl.BlockSpec((1, n_sb, GROUP_WIDTH), lambda b, *_: (b, 0, 0))
    run_spec = pl.BlockSpec((1, n_sb, LANES), lambda b, *_: (b, 0, 0))
    out_spec = pl.BlockSpec((1, tq, GROUP_WIDTH), lambda b, *_: (b, 0, 0))
    out = jax.ShapeDtypeStruct((db, tq, GROUP_WIDTH), BF16)
    sb_acc, sb_run, m_da = pl.pallas_call(
        functools.partial(_sample_kernel, page=page, tq=tq, n_step_pages=n_step_pages, n_sb_pages=n_fast,
                          lam_init=lam_init),
        grid_spec=pltpu.PrefetchScalarGridSpec(
            num_scalar_prefetch=1,
            grid=(db, n_pages // n_step_pages),
            in_specs=[new_spec(a) for a in new] + [small(lam_vecs), small(subln_g)] + page_specs,
            out_specs=[acc_spec, run_spec, out_spec],
            scratch_shapes=[
                pltpu.VMEM((n_sb, GROUP_WIDTH), BF16), pltpu.VMEM((n_da, LANES), BF16),
                pltpu.VMEM((n_sb, GROUP_WIDTH), F32), pltpu.VMEM((n_sb, 1), F32),
                pltpu.VMEM((n_da, LANES), F32), pltpu.VMEM((n_da, LANES), F32), pltpu.VMEM((n_da, LANES), F32),
            ]),
        out_shape=[jax.ShapeDtypeStruct((db, n_sb, GROUP_WIDTH), F32),
                   jax.ShapeDtypeStruct((db, n_sb, LANES), F32), out],
        compiler_params=_cparams(("arbitrary", "arbitrary")),
        name="sample_attention",
    )(page_table, *new, lam_vecs, subln_g, *page_args)

    dead = (jnp.min(sb_run, axis=(1, 2)) >= EXP_ZERO_ABOVE).astype(jnp.int32)
    n_tail = n_pages - n_fast

    def tail_spec(i):
        def index(b, pt, dead):
            return (layer, jnp.where(dead[b] == 1, 0, pt[b, n_pages - 1 - (n_fast + i)]), 0, 0)
        return pl.BlockSpec(page_block, index)

    m_sb = pl.pallas_call(
        functools.partial(_sb_tail_kernel, page=page, tq=tq),
        grid_spec=pltpu.PrefetchScalarGridSpec(
            num_scalar_prefetch=2,
            grid=(db,),
            in_specs=[acc_spec, run_spec, new_spec(sbq), new_spec(sbg)]
                     + [tail_spec(i) for i in range(n_tail) for _ in range(2)],
            out_specs=out_spec,
            scratch_shapes=[pltpu.VMEM((n_sb, GROUP_WIDTH), BF16), pltpu.VMEM((n_sb, GROUP_WIDTH), F32),
                            pltpu.VMEM((n_sb, 1), F32)]),
        out_shape=out,
        compiler_params=_cparams(("arbitrary",)),
        name="sample_sb_tail",
    )(page_table, dead, sb_acc, sb_run, sbq, sbg, *[c for _ in range(n_tail) for c in caches[:2]])
    return m_sb, m_da


def _outproj_kernel(x_ref, msb_ref, mda_ref, w_ref, gate_ref, fg_ref, o_ref, *, final):
    half = msb_ref.shape[1]
    y = (jnp.dot(msb_ref[...], w_ref[:half], preferred_element_type=F32)
         + jnp.dot(mda_ref[...], w_ref[half:], preferred_element_type=F32))
    y = x_ref[...] + gate_ref[...] * y
    if final:
        y = y * lax.rsqrt(jnp.mean(y * y, axis=-1, keepdims=True) + EPS) * fg_ref[...]
    o_ref[...] = y


def _outproj(x, m_sb, m_da, w16, gate, final_g, *, tm, rows_per_group, final):
    rows, d = x.shape
    if rows_per_group is None:
        gate_spec = pl.BlockSpec((tm, d), lambda i: (i, 0))
    else:
        per = rows_per_group // tm
        gate_spec = pl.BlockSpec((None, 1, d), lambda i: (i // per, 0, 0))
    row_spec = pl.BlockSpec((tm, d), lambda i: (i, 0))
    mix_spec = pl.BlockSpec((tm, GROUP_WIDTH), lambda i: (i, 0))
    return pl.pallas_call(
        functools.partial(_outproj_kernel, final=final),
        grid=(rows // tm,),
        in_specs=[row_spec, mix_spec, mix_spec, pl.BlockSpec(w16.shape, lambda i: (0, 0)),
                  gate_spec, pl.BlockSpec((1, d), lambda i: (0, 0))],
        out_specs=row_spec,
        out_shape=jax.ShapeDtypeStruct((rows, d), F32),
        compiler_params=_cparams(("arbitrary",)),
        name="outproj",
    )(x, m_sb, m_da, w16, gate, final_g)


def _rope_tables(pos):
    inv_freq = ROPE_THETA ** (-jnp.arange(0, HEAD_DIM, 2, dtype=F32) / HEAD_DIM)
    ang = pos.astype(F32)[:, None] * inv_freq[None, :]
    ang = jnp.concatenate([ang, ang, ang, ang], axis=-1)
    first_half = (jnp.arange(LANES) % HEAD_DIM) < HEAD_DIM // 2
    sin = jnp.sin(ang)
    return jnp.cos(ang), jnp.where(first_half, -sin, 0.0), jnp.where(first_half, 0.0, sin)


def _pick_tile(rows, target):
    tm = min(rows, target)
    while rows % tm:
        tm //= 2
    return tm


def kernel(x_prompt, x_sample, c_prompt, c_sample, cache_sb_k, cache_sb_v, cache_da_k, cache_da_v,
           page_table, w_mod, b_mod, norm_g, w_in, w_out, lambda_q1, lambda_k1, lambda_q2, lambda_k2,
           subln_g, final_norm_g):
    bsz, seq, d = x_prompt.shape
    db, dseq, _ = x_sample.shape
    depth = w_in.shape[0]
    n_pool, page = cache_sb_k.shape[1], cache_sb_k.shape[2]
    n_pages = page_table.shape[1]
    past_len = n_pages * page

    n_c = bsz + db
    c_all = jnp.concatenate([c_prompt, c_sample, jnp.zeros((-n_c % 8, d), F32)], axis=0)
    mod = _modulation(c_all, w_mod, b_mod)

    tabs_p = _rope_tables(jnp.arange(seq, dtype=jnp.int32))
    tabs_s = tuple(jnp.tile(t, (db, 1))
                   for t in _rope_tables(past_len + jnp.arange(dseq, dtype=jnp.int32)))

    caches = tuple(jnp.transpose(c, (0, 1, 3, 4, 2)).reshape(depth, n_pool, GROUP_WIDTH, page)
                   for c in (cache_sb_k, cache_sb_v))
    caches += tuple(c.reshape(depth, n_pool, page * DA_HEADS, 2 * HEAD_DIM)
                    for c in (cache_da_k, cache_da_v))
    w_in16 = w_in.astype(BF16)
    w_out16 = w_out.astype(BF16)
    lam_all = jnp.stack([lambda_q1, lambda_k1, lambda_q2, lambda_k2], axis=1)
    fg = final_norm_g.reshape(1, d)

    rows_p, rows_s = bsz * seq, db * dseq
    tm_p = _pick_tile(seq, 512)
    tm_out = _pick_tile(seq, 512)
    tm_s = _pick_tile(rows_s, 256)
    sb_blk = _pick_tile(seq, SB_BLOCK)
    da_blk = _pick_tile(seq, DA_BLOCK)

    hp = x_prompt.reshape(rows_p, d)
    hs = x_sample.reshape(rows_s, d)
    new_p, new_s = [], []
    for l in range(depth):
        lam_init = 0.8 - 0.6 * math.exp(-0.3 * l)
        g_l = norm_g[l].reshape(1, d)
        subln = subln_g[l].reshape(1, LANES)
        final = l == depth - 1

        def per_row(v):
            return jnp.broadcast_to(v[:, None, :], (db, dseq, d)).reshape(rows_s, d)

        shift_p, scale_p, gate_p = (mod[l, j, :bsz].reshape(bsz, 1, d) for j in range(3))
        shift_s, scale_s, gate_s = (per_row(mod[l, j, bsz:n_c]) for j in range(3))

        (sbq, sbk, sbv, sbg, daq, dak, dav, dag, sbk16, sbv16, dak16, dav16) = _inproj(
            hp, scale_p, shift_p, g_l, w_in16[l], *tabs_p, tm=tm_p, rows_per_group=seq)
        new_p.append((sbk, sbv, dak, dav))
        r3 = lambda a: a.reshape(bsz, seq, GROUP_WIDTH)
        m_sb = _prompt_attention(functools.partial(_sb_prompt_kernel, blk=sb_blk), "sb_prompt",
                                 r3(sbq), r3(sbk16), r3(sbv16), r3(sbg), (), [], blk=sb_blk)
        da_scratch = [pltpu.VMEM((2 * da_blk, da_blk), F32), pltpu.VMEM((2 * da_blk, da_blk), F32),
                      pltpu.VMEM((2 * da_blk, LANES), F32), pltpu.VMEM((2 * da_blk, LANES), F32),
                      pltpu.VMEM((2 * da_blk, LANES), F32)]
        m_da = _prompt_attention(functools.partial(_da_prompt_kernel, blk=da_blk, lam_init=lam_init),
                                 "da_prompt", r3(daq), r3(dak16), r3(dav16), r3(dag),
                                 (lam_all[l], subln), da_scratch, blk=da_blk)
        hp = _outproj(hp, m_sb.reshape(rows_p, GROUP_WIDTH), m_da.reshape(rows_p, GROUP_WIDTH),
                      w_out16[l], gate_p, fg, tm=tm_out, rows_per_group=seq, final=final)

        (sbq, sbk, sbv, sbg, daq, dak, dav, dag, sbk16, sbv16, dak16, dav16) = _inproj(
            hs, scale_s, shift_s, g_l, w_in16[l], *tabs_s, tm=tm_s, rows_per_group=None)
        new_s.append((sbk, sbv, dak, dav))
        s3 = lambda a: a.reshape(db, dseq, GROUP_WIDTH)
        da_rows = lambda a: a.reshape(db, dseq * DA_HEADS, 2 * HEAD_DIM)
        m_sb, m_da = _sample_attention(
            page_table, (s3(sbq), s3(sbk16), s3(sbv16), s3(daq), da_rows(dak16), da_rows(dav16), s3(dag)),
            s3(sbg), caches, l, lam_all[l], subln, lam_init=lam_init)
        hs = _outproj(hs, m_sb.reshape(rows_s, GROUP_WIDTH), m_da.reshape(rows_s, GROUP_WIDTH),
                      w_out16[l], gate_s, fg, tm=tm_s, rows_per_group=None, final=final)

    def stacked(rows, idx):
        return jnp.stack([r[idx] for r in rows], axis=0)

    sb_heads = (SB_HEADS, HEAD_DIM)
    da_heads = (DA_HEADS, 2 * HEAD_DIM)
    outs_p = [jnp.transpose(stacked(new_p, i).reshape((depth, bsz) + sb_heads + (seq,)), (0, 1, 4, 2, 3))
              for i in range(2)]
    outs_p += [stacked(new_p, i).reshape((depth, bsz, seq) + da_heads) for i in range(2, 4)]
    outs_s = [stacked(new_s, i).reshape((depth, db, dseq) + (sb_heads if i < 2 else da_heads))
              for i in range(4)]
    return (hp.reshape(bsz, seq, d), hs.reshape(db, dseq, d), *outs_p, *outs_s)
```

```python
import functools
import math

import jax
import jax.numpy as jnp
from jax import lax
from jax.experimental import pallas as pl
from jax.experimental.pallas import tpu as pltpu

F32 = jnp.float32
BF16 = jnp.bfloat16

HEAD_DIM = 64
LANES = 128
SLABS = 4
GROUP_WIDTH = SLABS * LANES
SB_HEADS = GROUP_WIDTH // HEAD_DIM
DA_HEADS = GROUP_WIDTH // (2 * HEAD_DIM)
ROPE_THETA = 10000.0
EPS = 1e-6
Q_SCALE = HEAD_DIM ** -0.5
LOG2E = math.log2(math.e)
EXP_ZERO_ABOVE = 104.0
VMEM_LIMIT = 48 * 1024 * 1024
SB_BLOCK = 256
DA_BLOCK = 512
PAGES_PER_STEP = 16
SB_PAGE_GROUP = 2
SB_FAST_PAGES = 4


def _cparams(sem):
    return pltpu.CompilerParams(dimension_semantics=sem, vmem_limit_bytes=VMEM_LIMIT)


def _mod_kernel(c_ref, w_ref, b_ref, o_ref):
    c = c_ref[...]
    sc = c / (1.0 + jnp.exp(-c))
    o_ref[0, 0] = jnp.dot(sc, w_ref[0], precision=lax.Precision.HIGHEST,
                          preferred_element_type=F32) + b_ref[0, 0]


def _modulation(c_all, w_mod, b_mod):
    depth, d, _ = w_mod.shape
    r = c_all.shape[0]
    return pl.pallas_call(
        _mod_kernel,
        grid=(depth, 3),
        in_specs=[
            pl.BlockSpec((r, d), lambda l, j: (0, 0)),
            pl.BlockSpec((1, d, d), lambda l, j: (l, 0, j)),
            pl.BlockSpec((1, 1, 1, d), lambda l, j: (l, j, 0, 0)),
        ],
        out_specs=pl.BlockSpec((1, 1, r, d), lambda l, j: (l, j, 0, 0)),
        out_shape=jax.ShapeDtypeStruct((depth, 3, r, d), F32),
        compiler_params=_cparams(("arbitrary", "arbitrary")),
        name="modulation",
    )(c_all, w_mod, b_mod.reshape(depth, 3, 1, d))


def _inproj_kernel(x_ref, scale_ref, shift_ref, g_ref, w_ref, cos_ref, sin_up_ref, sin_dn_ref,
                   sbq_ref, sbk_ref, sbv_ref, sbg_ref, daq_ref, dak_ref, dav_ref, dag_ref,
                   sbk16_ref, sbv16_ref, dak16_ref, dav16_ref, *, sb_transposed):
    x = x_ref[...]
    tm = x.shape[0]
    xn = x * lax.rsqrt(jnp.mean(x * x, axis=-1, keepdims=True) + EPS)
    h = (xn * g_ref[...]) * (1.0 + scale_ref[...]) + shift_ref[...]
    h16 = h.astype(BF16)

    def seg(i):
        return jnp.dot(h16, w_ref[:, i * GROUP_WIDTH:(i + 1) * GROUP_WIDTH],
                       preferred_element_type=F32)

    def rope_slab(p):
        return (p * cos_ref[...] + pltpu.roll(p, LANES - HEAD_DIM // 2, 1) * sin_up_ref[...]
                + pltpu.roll(p, HEAD_DIM // 2, 1) * sin_dn_ref[...])

    sbq_ref[...] = (seg(0) * Q_SCALE).astype(BF16)
    k = seg(1)
    if sb_transposed:
        sbk_ref[0] = k.T
    else:
        sbk_ref[...] = k
    sbk16_ref[...] = k.astype(BF16)
    v = seg(2)
    if sb_transposed:
        sbv_ref[0] = v.T
    else:
        sbv_ref[...] = v
    sbv16_ref[...] = v.astype(BF16)
    g = seg(3)
    sbg_ref[...] = g / (1.0 + jnp.exp(-g))
    q = seg(4)
    for s in range(SLABS):
        sl = slice(s * LANES, (s + 1) * LANES)
        daq_ref[:, sl] = (rope_slab(q[:, sl]) * (Q_SCALE * LOG2E)).astype(BF16)
    k = seg(5)
    for s in range(SLABS):
        sl = slice(s * LANES, (s + 1) * LANES)
        kr = rope_slab(k[:, sl])
        dak_ref[pl.ds(s, tm, stride=SLABS), :] = kr
        dak16_ref[:, sl] = kr.astype(BF16)
    v = seg(6)
    for s in range(SLABS):
        dav_ref[pl.ds(s, tm, stride=SLABS), :] = v[:, s * LANES:(s + 1) * LANES]
    dav16_ref[...] = v.astype(BF16)
    g = seg(7)
    dag_ref[...] = g / (1.0 + jnp.exp(-g))


def _inproj(x, scale, shift, norm_g, w16, cos_t, sin_up, sin_dn, *, tm, rows_per_group):
    rows, d = x.shape
    nblk = rows // tm
    if rows_per_group is None:
        mod_spec = pl.BlockSpec((tm, d), lambda i: (i, 0))
    else:
        per = rows_per_group // tm
        mod_spec = pl.BlockSpec((None, 1, d), lambda i: (i // per, 0, 0))
    tab_blocks = cos_t.shape[0] // tm
    tab_spec = pl.BlockSpec((tm, LANES), lambda i: (i % tab_blocks, 0))
    out_spec = pl.BlockSpec((tm, GROUP_WIDTH), lambda i: (i, 0))
    f32o = jax.ShapeDtypeStruct((rows, GROUP_WIDTH), F32)
    b16o = jax.ShapeDtypeStruct((rows, GROUP_WIDTH), BF16)
    da_spec = pl.BlockSpec((tm * SLABS, LANES), lambda i: (i, 0))
    da_o = jax.ShapeDtypeStruct((rows * SLABS, LANES), F32)
    if rows_per_group is None:
        sb_spec, sb_o = out_spec, f32o
    else:
        sb_spec = pl.BlockSpec((1, GROUP_WIDTH, tm), lambda i: (i // per, 0, i % per))
        sb_o = jax.ShapeDtypeStruct((rows // rows_per_group, GROUP_WIDTH, rows_per_group), F32)
    return pl.pallas_call(
        functools.partial(_inproj_kernel, sb_transposed=rows_per_group is not None),
        grid=(nblk,),
        in_specs=[
            pl.BlockSpec((tm, d), lambda i: (i, 0)),
            mod_spec, mod_spec,
            pl.BlockSpec((1, d), lambda i: (0, 0)),
            pl.BlockSpec(w16.shape, lambda i: (0, 0)),
            tab_spec, tab_spec, tab_spec,
        ],
        out_specs=[out_spec, sb_spec, sb_spec, out_spec, out_spec, da_spec, da_spec, out_spec] + [out_spec] * 4,
        out_shape=[b16o, sb_o, sb_o, f32o, b16o, da_o, da_o, f32o, b16o, b16o, b16o, b16o],
        compiler_params=_cparams(("arbitrary",)),
        name="inproj",
    )(x, scale, shift, norm_g, w16, cos_t, sin_up, sin_dn)


def _nt_dot(a, b):
    return lax.dot_general(a, b, (((1,), (1,)), ((), ())), preferred_element_type=F32)


def _suffix_matrix(tk):
    j = lax.broadcasted_iota(jnp.int32, (tk, tk), 0)
    s = lax.broadcasted_iota(jnp.int32, (tk, tk), 1)
    return jnp.where(j > s, 1.0, 0.0).astype(BF16)


def _sb_weights(z, run, valid, upper):
    e = jnp.log(1.0 + jnp.exp(-jnp.abs(z)))
    sp = jnp.maximum(z, 0.0) + e
    ls = jnp.minimum(z, 0.0) - e
    if valid is not None:
        sp = jnp.where(valid, sp, 0.0)
    sp_hi = sp.astype(BF16)
    sp_lo = (sp - sp_hi.astype(F32)).astype(BF16)
    after = (jnp.dot(sp_hi, upper, preferred_element_type=F32)
             + jnp.dot(sp_lo, upper, preferred_element_type=F32))
    a = jnp.exp(ls - (after + run))
    if valid is not None:
        a = jnp.where(valid, a, 0.0)
    return a.astype(BF16), run + jnp.sum(sp, axis=-1, keepdims=True)


def _lane_chunks(x):
    return [x[:, c * LANES:(c + 1) * LANES] for c in range(x.shape[1] // LANES)]


def _da_weights(tiles, m, l, valid):
    if valid is not None:
        tiles = [jnp.where(valid, s, -jnp.inf) for s in tiles]
    chunks = [_lane_chunks(s) for s in tiles]
    top = chunks[0][0]
    for c in [c for cs in chunks for c in cs][1:]:
        top = jnp.maximum(top, c)
    m_new = jnp.maximum(m, jnp.broadcast_to(jnp.max(top, axis=-1, keepdims=True), m.shape))
    alpha = jnp.exp2(m - m_new)
    l_new = alpha * l
    ps = []
    for cs in chunks:
        pcs = [jnp.exp2(c - m_new) for c in cs]
        for pc in pcs:
            l_new = l_new + pc
        ps.append(jnp.concatenate([pc.astype(BF16) for pc in pcs], axis=1))
    return m_new, alpha, ps, l_new


def _lambda(lam_ref, lam_init):
    lam = lam_ref[...]
    s1 = jnp.sum(lam[0:1] * lam[1:2], axis=-1, keepdims=True)
    s2 = jnp.sum(lam[2:3] * lam[3:4], axis=-1, keepdims=True)
    return jnp.exp(s1) - jnp.exp(s2) + lam_init


def _da_finish(acc1, l1, acc2, l2, lam, subln_g, lam_init, gate):
    o = (acc1 / jnp.sum(l1, axis=-1, keepdims=True)
         - lam * (acc2 / jnp.sum(l2, axis=-1, keepdims=True)))
    on = o * lax.rsqrt(jnp.mean(o * o, axis=-1, keepdims=True) + EPS)
    return ((on * subln_g) * (1.0 - lam_init) * gate).astype(BF16)


def _stack_queries(q):
    lane = lax.broadcasted_iota(jnp.int32, q.shape, 1)
    zero = jnp.zeros_like(q)
    return jnp.concatenate([jnp.where(lane < HEAD_DIM, q, zero),
                            jnp.where(lane < HEAD_DIM, zero, q)], axis=0)


def _row_col(m, tk, tq):
    row = lax.broadcasted_iota(jnp.int32, (m, tk), 0)
    col = lax.broadcasted_iota(jnp.int32, (m, tk), 1)
    return jnp.where(row >= tq, row - tq, row), col


def _sb_prompt_kernel(q_ref, k_ref, v_ref, g_ref, o_ref, *, blk):
    i = pl.program_id(2)
    q2 = _stack_queries(q_ref[0])
    upper = _suffix_matrix(blk)
    t, s = _row_col(2 * blk, blk, blk)

    def kv(j):
        start = pl.multiple_of(j * blk, blk)
        return k_ref[0, pl.ds(start, blk), :], v_ref[0, pl.ds(start, blk), :]

    def block(j, run, valid):
        k, v = kv(j)
        a, run = _sb_weights(_nt_dot(q2, k), run, valid, upper)
        return jnp.dot(a, v, preferred_element_type=F32), run

    def pair(j, run, valid_hi, valid_lo):
        pv_hi, run = block(j, run, valid_hi)
        pv_lo, run = block(jnp.maximum(j - 1, 0), run, valid_lo)
        return pv_hi + pv_lo, run

    def live(run):
        return jnp.min(run) < EXP_ZERO_ABOVE

    acc, run = pair(i, jnp.zeros((2 * blk, 1), F32), s < t, s >= jnp.where(i >= 1, 0, blk))

    def cond(c):
        j, _, run = c
        return jnp.logical_and(j >= 1, live(run))

    def body(c):
        j, acc, run = c
        pv, run = pair(j, run, None, None)
        return j - 2, acc + pv, run

    j, acc, run = lax.while_loop(cond, body, (i - 2, acc, run))

    def last_block(acc, run):
        pv, run = block(0, run, None)
        return acc + pv, run

    acc, _ = lax.cond(jnp.logical_and(j == 0, live(run)), last_block, lambda acc, run: (acc, run), acc, run)
    lane = lax.broadcasted_iota(jnp.int32, (blk, LANES), 1)
    o = jnp.where(lane < HEAD_DIM, acc[:blk], acc[blk:])
    o_ref[0] = (o * g_ref[0]).astype(BF16)


def _da_prompt_kernel(q_ref, k_ref, v_ref, g_ref, lam_ref, subln_ref, o_ref,
                      s_even, s_odd, m_s, l_s, acc_s, *, blk, lam_init):
    i = pl.program_id(2)
    q2 = _stack_queries(q_ref[0])
    m2 = 2 * blk

    def scores_into(s_ref, j):
        start = pl.multiple_of(j * blk, blk)
        s_ref[...] = _nt_dot(q2, k_ref[0, pl.ds(start, blk), :])

    def absorb(s_ref, j, valid):
        start = pl.multiple_of(j * blk, blk)
        m, alpha, (p,), l = _da_weights([s_ref[...]], m_s[...], l_s[...], valid)
        m_s[...] = m
        l_s[...] = l
        acc_s[...] = alpha * acc_s[...] + jnp.dot(p, v_ref[0, pl.ds(start, blk), :],
                                                  preferred_element_type=F32)

    m_s[...] = jnp.full((m2, LANES), -jnp.inf, F32)
    l_s[...] = jnp.zeros((m2, LANES), F32)
    acc_s[...] = jnp.zeros((m2, LANES), F32)
    scores_into(s_even, 0)

    def pair(jj, carry):
        j = 2 * jj
        scores_into(s_odd, j + 1)
        absorb(s_even, j, None)
        scores_into(s_even, j + 2)
        absorb(s_odd, j + 1, None)
        return carry

    lax.fori_loop(0, i // 2, pair, 0)
    t, s = _row_col(m2, blk, blk)
    diag = s <= t

    @pl.when(i % 2 == 0)
    def _():
        absorb(s_even, i, diag)

    @pl.when(i % 2 == 1)
    def _():
        scores_into(s_odd, i)
        absorb(s_even, i - 1, None)
        absorb(s_odd, i, diag)

    acc, l = acc_s[...], l_s[...]
    o_ref[0] = _da_finish(acc[:blk], l[:blk], acc[blk:], l[blk:], _lambda(lam_ref, lam_init),
                          subln_ref[...], lam_init, g_ref[0])


def _prompt_attention(kernel, name, q, k, v, g, extra, scratch, *, blk):
    b, t, _ = q.shape
    qspec = pl.BlockSpec((1, blk, LANES), lambda bi, s, i: (bi, i, s))
    kvspec = pl.BlockSpec((1, t, LANES), lambda bi, s, i: (bi, 0, s))
    extra_specs = [pl.BlockSpec(e.shape, lambda bi, s, i: (0, 0)) for e in extra]
    return pl.pallas_call(
        kernel,
        grid=(b, SLABS, t // blk),
        in_specs=[qspec, kvspec, kvspec, qspec] + extra_specs,
        out_specs=qspec,
        out_shape=jax.ShapeDtypeStruct((b, t, GROUP_WIDTH), BF16),
        scratch_shapes=scratch,
        compiler_params=_cparams(("arbitrary", "arbitrary", "arbitrary")),
        name=name,
    )(q, k, v, g, *extra)


def _block_diag_queries(q, tq):
    n_sb = SB_HEADS * tq
    q_rows = jnp.concatenate([q] * SB_HEADS, axis=0)
    own = (lax.broadcasted_iota(jnp.int32, (n_sb, GROUP_WIDTH), 1) // HEAD_DIM
           == lax.broadcasted_iota(jnp.int32, (n_sb, GROUP_WIDTH), 0) // tq)
    return jnp.where(own, q_rows, 0.0).astype(BF16)


def _sb_page_groups(kv_refs, qbd_ref, acc_ref, run_ref, upper):
    def groups(g0):
        if g0 >= len(kv_refs):
            return

        @pl.when(jnp.min(run_ref[...]) < EXP_ZERO_ABOVE)
        def _():
            run = run_ref[...]
            acc = acc_ref[...]
            for k_ref, v_ref in kv_refs[g0:g0 + SB_PAGE_GROUP]:
                kt = k_ref[0, 0].astype(BF16)
                vt = v_ref[0, 0].astype(BF16)
                a, run = _sb_weights(jnp.dot(qbd_ref[...], kt, preferred_element_type=F32), run, None, upper)
                acc = acc + _nt_dot(a, vt)
            acc_ref[...] = acc
            run_ref[...] = run
            groups(g0 + SB_PAGE_GROUP)

    groups(0)


def _sb_sample_output(acc, gate, tq):
    col_head = lax.broadcasted_iota(jnp.int32, (tq, GROUP_WIDTH), 1) // HEAD_DIM
    o = jnp.zeros((tq, GROUP_WIDTH), F32)
    for h in range(SB_HEADS):
        o = o + jnp.where(col_head == h, acc[h * tq:(h + 1) * tq], 0.0)
    return (o * gate).astype(BF16)


def _sample_kernel(pt_ref, sbq_ref, sbkn_ref, sbvn_ref, sbg_ref, daq_ref, dakn_ref, davn_ref, dag_ref,
                   lam_ref, subln_ref, *rest, page, tq, n_step_pages, n_sb_pages, lam_init):
    del pt_ref
    sb_pages = [(rest[2 * i], rest[2 * i + 1]) for i in range(n_sb_pages)]
    rest = rest[2 * n_sb_pages:]
    da_pages = [(rest[2 * i], rest[2 * i + 1]) for i in range(n_step_pages)]
    osb_acc_ref, osb_run_ref, osb_ref, oda_ref = rest[2 * n_step_pages:2 * n_step_pages + 4]
    qbd_s, q2_s, sb_acc, sb_run, da_m, da_l, da_acc = rest[2 * n_step_pages + 4:]
    step = pl.program_id(1)
    last = pl.num_programs(1) - 1
    n_sb = SB_HEADS * tq
    n_da = 2 * DA_HEADS * tq
    upper = _suffix_matrix(page)

    def iota(shape, axis):
        return lax.broadcasted_iota(jnp.int32, shape, axis)

    da_head_of_row = iota((n_da, DA_HEADS * page), 0) // (2 * tq)
    da_page_valid = (iota((n_da, DA_HEADS * page), 1) % DA_HEADS) == da_head_of_row

    def pad_rows(x):
        return jnp.concatenate([x, jnp.zeros((page - x.shape[0], x.shape[1]), x.dtype)], axis=0)

    @pl.when(step == 0)
    def _():
        qbd = _block_diag_queries(sbq_ref[0].astype(F32), tq)
        qbd_s[...] = qbd
        q = daq_ref[0].astype(F32)
        lane = iota((tq, LANES), 1)
        pieces = []
        for h in range(DA_HEADS):
            slab = q[:, h * LANES:(h + 1) * LANES]
            pieces += [jnp.where(lane < HEAD_DIM, slab, 0.0), jnp.where(lane < HEAD_DIM, 0.0, slab)]
        q2 = jnp.concatenate(pieces, axis=0).astype(BF16)
        q2_s[...] = q2

        t_sb = iota((n_sb, page), 0) % tq
        a, run = _sb_weights(_nt_dot(qbd, pad_rows(sbkn_ref[0])), jnp.zeros((n_sb, 1), F32),
                             iota((n_sb, page), 1) < t_sb, upper)
        sb_acc[...] = jnp.dot(a, pad_rows(sbvn_ref[0]), preferred_element_type=F32)
        sb_run[...] = run

        c = iota((n_da, page), 1)
        r = iota((n_da, page), 0)
        valid = jnp.logical_and((c % DA_HEADS) == r // (2 * tq), c // DA_HEADS <= r % tq)
        m, _, (p,), l = _da_weights([_nt_dot(q2, pad_rows(dakn_ref[0]))], jnp.full((n_da, LANES), -jnp.inf, F32),
                                    jnp.zeros((n_da, LANES), F32), valid)
        da_m[...] = m
        da_l[...] = l
        da_acc[...] = jnp.dot(p, pad_rows(davn_ref[0]), preferred_element_type=F32)

    @pl.when(step == 0)
    def _():
        _sb_page_groups(sb_pages, qbd_s, sb_acc, sb_run, upper)

    tiles = [_nt_dot(q2_s[...], k_ref[0, 0].astype(BF16))
             for k_ref, _ in da_pages]
    m, alpha, ps, l = _da_weights(tiles, da_m[...], da_l[...], da_page_valid)
    acc = alpha * da_acc[...]
    for (_, v_ref), p in zip(da_pages, ps):
        acc = acc + jnp.dot(p, v_ref[0, 0].astype(BF16), preferred_element_type=F32)
    da_m[...] = m
    da_l[...] = l
    da_acc[...] = acc

    @pl.when(step == last)
    def _():
        osb_acc_ref[0] = sb_acc[...]
        osb_run_ref[0] = jnp.broadcast_to(sb_run[...], (n_sb, LANES))
        osb_ref[0] = _sb_sample_output(sb_acc[...], sbg_ref[0], tq)

        lam = _lambda(lam_ref, lam_init)
        acc, l = da_acc[...], da_l[...]
        for h in range(DA_HEADS):
            r1, r2 = 2 * h * tq, (2 * h + 1) * tq
            oda_ref[0, :, h * LANES:(h + 1) * LANES] = _da_finish(
                acc[r1:r1 + tq], l[r1:r1 + tq], acc[r2:r2 + tq], l[r2:r2 + tq], lam, subln_ref[...],
                lam_init, dag_ref[0, :, h * LANES:(h + 1) * LANES])


def _sb_tail_kernel(pt_ref, dead_ref, acc_ref, run_ref, sbq_ref, sbg_ref, *rest, page, tq):
    del pt_ref
    n_tail = (len(rest) - 4) // 2
    kv_refs = [(rest[2 * i], rest[2 * i + 1]) for i in range(n_tail)]
    osb_ref, qbd_s, acc_s, run_s = rest[2 * n_tail:]
    acc_s[...] = acc_ref[0]
    run_s[...] = run_ref[0][:, :1]

    @pl.when(dead_ref[pl.program_id(0)] == 0)
    def _():
        qbd_s[...] = _block_diag_queries(sbq_ref[0].astype(F32), tq)
        _sb_page_groups(kv_refs, qbd_s, acc_s, run_s, _suffix_matrix(page))

    osb_ref[0] = _sb_sample_output(acc_s[...], sbg_ref[0], tq)


def _sample_attention(page_table, new, sbg, caches, layer, lam_vecs, subln_g, *, lam_init):
    sbq = new[0]
    db, tq, _ = sbq.shape
    n_pages = page_table.shape[1]
    page = caches[0].shape[3]
    n_step_pages = math.gcd(PAGES_PER_STEP, n_pages)
    n_fast = min(SB_FAST_PAGES, n_step_pages)
    n_sb, n_da = SB_HEADS * tq, 2 * DA_HEADS * tq
    page_block = (1, 1, GROUP_WIDTH, page)

    def new_spec(a):
        return pl.BlockSpec((1,) + a.shape[1:], lambda b, *_: (b, 0, 0))

    def small(e):
        return pl.BlockSpec(e.shape, lambda *_: (0, 0))

    def sb_fast_spec(i):
        return pl.BlockSpec(page_block, lambda b, st, pt: (layer, pt[b, n_pages - 1 - i], 0, 0))

    def da_spec(i):
        return pl.BlockSpec(page_block,
                            lambda b, st, pt: (layer, pt[b, n_pages - 1 - (st * n_step_pages + i)], 0, 0))

    page_specs = [sb_fast_spec(i) for i in range(n_fast) for _ in range(2)]
    page_args = [c for _ in range(n_fast) for c in caches[:2]]
    page_specs += [da_spec(i) for i in range(n_step_pages) for _ in range(2)]
    page_args += [c for _ in range(n_step_pages) for c in caches[2:]]
    acc_spec = pl.BlockSpec((1, n_sb, GROUP_WIDTH), lambda b, *_: (b, 0, 0))
    run_spec = pl.BlockSpec((1, n_sb, LANES), lambda b, *_: (b, 0, 0))
    out_spec = pl.BlockSpec((1, tq, GROUP_WIDTH), lambda b, *_: (b, 0, 0))
    out = jax.ShapeDtypeStruct((db, tq, GROUP_WIDTH), BF16)
    sb_acc, sb_run, m_sb_fast, m_da = pl.pallas_call(
        functools.partial(_sample_kernel, page=page, tq=tq, n_step_pages=n_step_pages, n_sb_pages=n_fast,
                          lam_init=lam_init),
        grid_spec=pltpu.PrefetchScalarGridSpec(
            num_scalar_prefetch=1,
            grid=(db, n_pages // n_step_pages),
            in_specs=[new_spec(a) for a in new] + [small(lam_vecs), small(subln_g)] + page_specs,
            out_specs=[acc_spec, run_spec, out_spec, out_spec],
            scratch_shapes=[
                pltpu.VMEM((n_sb, GROUP_WIDTH), BF16), pltpu.VMEM((n_da, LANES), BF16),
                pltpu.VMEM((n_sb, GROUP_WIDTH), F32), pltpu.VMEM((n_sb, 1), F32),
                pltpu.VMEM((n_da, LANES), F32), pltpu.VMEM((n_da, LANES), F32), pltpu.VMEM((n_da, LANES), F32),
            ]),
        out_shape=[jax.ShapeDtypeStruct((db, n_sb, GROUP_WIDTH), F32),
                   jax.ShapeDtypeStruct((db, n_sb, LANES), F32), out, out],
        compiler_params=_cparams(("arbitrary", "arbitrary")),
        name="sample_attention",
    )(page_table, *new, lam_vecs, subln_g, *page_args)

    dead = (jnp.min(sb_run, axis=(1, 2)) >= EXP_ZERO_ABOVE).astype(jnp.int32)
    n_tail = n_pages - n_fast

    def tail_spec(i):
        def index(b, pt, dead):
            return (layer, jnp.where(dead[b] == 1, 0, pt[b, n_pages - 1 - (n_fast + i)]), 0, 0)
        return pl.BlockSpec(page_block, index)

    def tail(*sb_caches):
        return pl.pallas_call(
            functools.partial(_sb_tail_kernel, page=page, tq=tq),
            grid_spec=pltpu.PrefetchScalarGridSpec(
                num_scalar_prefetch=2,
                grid=(db,),
                in_specs=[acc_spec, run_spec, new_spec(sbq), new_spec(sbg)]
                         + [tail_spec(i) for i in range(n_tail) for _ in range(2)],
                out_specs=out_spec,
                scratch_shapes=[pltpu.VMEM((n_sb, GROUP_WIDTH), BF16), pltpu.VMEM((n_sb, GROUP_WIDTH), F32),
                                pltpu.VMEM((n_sb, 1), F32)]),
            out_shape=out,
            compiler_params=_cparams(("arbitrary",)),
            name="sample_sb_tail",
        )(page_table, dead, sb_acc, sb_run, sbq, sbg, *[c for _ in range(n_tail) for c in sb_caches])

    m_sb = lax.cond(jnp.all(dead == 1), lambda *_: m_sb_fast, tail, *caches[:2])
    return m_sb, m_da


def _outproj_kernel(x_ref, msb_ref, mda_ref, w_ref, gate_ref, fg_ref, o_ref, *, final):
    half = msb_ref.shape[1]
    y = (jnp.dot(msb_ref[...], w_ref[:half], preferred_element_type=F32)
         + jnp.dot(mda_ref[...], w_ref[half:], preferred_element_type=F32))
    y = x_ref[...] + gate_ref[...] * y
    if final:
        y = y * lax.rsqrt(jnp.mean(y * y, axis=-1, keepdims=True) + EPS) * fg_ref[...]
    o_ref[...] = y


def _outproj(x, m_sb, m_da, w16, gate, final_g, *, tm, rows_per_group, final):
    rows, d = x.shape
    if rows_per_group is None:
        gate_spec = pl.BlockSpec((tm, d), lambda i: (i, 0))
    else:
        per = rows_per_group // tm
        gate_spec = pl.BlockSpec((None, 1, d), lambda i: (i // per, 0, 0))
    row_spec = pl.BlockSpec((tm, d), lambda i: (i, 0))
    mix_spec = pl.BlockSpec((tm, GROUP_WIDTH), lambda i: (i, 0))
    return pl.pallas_call(
        functools.partial(_outproj_kernel, final=final),
        grid=(rows // tm,),
        in_specs=[row_spec, mix_spec, mix_spec, pl.BlockSpec(w16.shape, lambda i: (0, 0)),
                  gate_spec, pl.BlockSpec((1, d), lambda i: (0, 0))],
        out_specs=row_spec,
        out_shape=jax.ShapeDtypeStruct((rows, d), F32),
        compiler_params=_cparams(("arbitrary",)),
        name="outproj",
    )(x, m_sb, m_da, w16, gate, final_g)


def _rope_tables(pos):
    inv_freq = ROPE_THETA ** (-jnp.arange(0, HEAD_DIM, 2, dtype=F32) / HEAD_DIM)
    ang = pos.astype(F32)[:, None] * inv_freq[None, :]
    ang = jnp.concatenate([ang, ang, ang, ang], axis=-1)
    first_half = (jnp.arange(LANES) % HEAD_DIM) < HEAD_DIM // 2
    sin = jnp.sin(ang)
    return jnp.cos(ang), jnp.where(first_half, -sin, 0.0), jnp.where(first_half, 0.0, sin)


def _pick_tile(rows, target):
    tm = min(rows, target)
    while rows % tm:
        tm //= 2
    return tm


def kernel(x_prompt, x_sample, c_prompt, c_sample, cache_sb_k, cache_sb_v, cache_da_k, cache_da_v,
           page_table, w_mod, b_mod, norm_g, w_in, w_out, lambda_q1, lambda_k1, lambda_q2, lambda_k2,
           subln_g, final_norm_g):
    bsz, seq, d = x_prompt.shape
    db, dseq, _ = x_sample.shape
    depth = w_in.shape[0]
    n_pool, page = cache_sb_k.shape[1], cache_sb_k.shape[2]
    n_pages = page_table.shape[1]
    past_len = n_pages * page

    n_c = bsz + db
    c_all = jnp.concatenate([c_prompt, c_sample, jnp.zeros((-n_c % 8, d), F32)], axis=0)
    mod = _modulation(c_all, w_mod, b_mod)

    tabs_p = _rope_tables(jnp.arange(seq, dtype=jnp.int32))
    tabs_s = tuple(jnp.tile(t, (db, 1))
                   for t in _rope_tables(past_len + jnp.arange(dseq, dtype=jnp.int32)))

    caches = tuple(jnp.transpose(c, (0, 1, 3, 4, 2)).reshape(depth, n_pool, GROUP_WIDTH, page)
                   for c in (cache_sb_k, cache_sb_v))
    caches += tuple(c.reshape(depth, n_pool, page * DA_HEADS, 2 * HEAD_DIM)
                    for c in (cache_da_k, cache_da_v))
    w_in16 = w_in.astype(BF16)
    w_out16 = w_out.astype(BF16)
    lam_all = jnp.stack([lambda_q1, lambda_k1, lambda_q2, lambda_k2], axis=1)
    fg = final_norm_g.reshape(1, d)

    rows_p, rows_s = bsz * seq, db * dseq
    tm_p = _pick_tile(seq, 512)
    tm_out = _pick_tile(seq, 512)
    tm_s = _pick_tile(rows_s, 256)
    sb_blk = _pick_tile(seq, SB_BLOCK)
    da_blk = _pick_tile(seq, DA_BLOCK)

    hp = x_prompt.reshape(rows_p, d)
    hs = x_sample.reshape(rows_s, d)
    new_p, new_s = [], []
    for l in range(depth):
        lam_init = 0.8 - 0.6 * math.exp(-0.3 * l)
        g_l = norm_g[l].reshape(1, d)
        subln = subln_g[l].reshape(1, LANES)
        final = l == depth - 1

        def per_row(v):
            return jnp.broadcast_to(v[:, None, :], (db, dseq, d)).reshape(rows_s, d)

        shift_p, scale_p, gate_p = (mod[l, j, :bsz].reshape(bsz, 1, d) for j in range(3))
        shift_s, scale_s, gate_s = (per_row(mod[l, j, bsz:n_c]) for j in range(3))

        (sbq, sbk, sbv, sbg, daq, dak, dav, dag, sbk16, sbv16, dak16, dav16) = _inproj(
            hp, scale_p, shift_p, g_l, w_in16[l], *tabs_p, tm=tm_p, rows_per_group=seq)
        new_p.append((sbk, sbv, dak, dav))
        r3 = lambda a: a.reshape(bsz, seq, GROUP_WIDTH)
        m_sb = _prompt_attention(functools.partial(_sb_prompt_kernel, blk=sb_blk), "sb_prompt",
                                 r3(sbq), r3(sbk16), r3(sbv16), r3(sbg), (), [], blk=sb_blk)
        da_scratch = [pltpu.VMEM((2 * da_blk, da_blk), F32), pltpu.VMEM((2 * da_blk, da_blk), F32),
                      pltpu.VMEM((2 * da_blk, LANES), F32), pltpu.VMEM((2 * da_blk, LANES), F32),
                      pltpu.VMEM((2 * da_blk, LANES), F32)]
        m_da = _prompt_attention(functools.partial(_da_prompt_kernel, blk=da_blk, lam_init=lam_init),
                                 "da_prompt", r3(daq), r3(dak16), r3(dav16), r3(dag),
                                 (lam_all[l], subln), da_scratch, blk=da_blk)
        hp = _outproj(hp, m_sb.reshape(rows_p, GROUP_WIDTH), m_da.reshape(rows_p, GROUP_WIDTH),
                      w_out16[l], gate_p, fg, tm=tm_out, rows_per_group=seq, final=final)

        (sbq, sbk, sbv, sbg, daq, dak, dav, dag, sbk16, sbv16, dak16, dav16) = _inproj(
            hs, scale_s, shift_s, g_l, w_in16[l], *tabs_s, tm=tm_s, rows_per_group=None)
        new_s.append((sbk, sbv, dak, dav))
        s3 = lambda a: a.reshape(db, dseq, GROUP_WIDTH)
        da_rows = lambda a: a.reshape(db, dseq * DA_HEADS, 2 * HEAD_DIM)
        m_sb, m_da = _sample_attention(
            page_table, (s3(sbq), s3(sbk16), s3(sbv16), s3(sbg), s3(daq), da_rows(dak16), da_rows(dav16),
                         s3(dag)),
            s3(sbg), caches, l, lam_all[l], subln, lam_init=lam_init)
        hs = _outproj(hs, m_sb.reshape(rows_s, GROUP_WIDTH), m_da.reshape(rows_s, GROUP_WIDTH),
                      w_out16[l], gate_s, fg, tm=tm_s, rows_per_group=None, final=final)

    def stacked(rows, idx):
        return jnp.stack([r[idx] for r in rows], axis=0)

    sb_heads = (SB_HEADS, HEAD_DIM)
    da_heads = (DA_HEADS, 2 * HEAD_DIM)
    outs_p = [jnp.transpose(stacked(new_p, i).reshape((depth, bsz) + sb_heads + (seq,)), (0, 1, 4, 2, 3))
              for i in range(2)]
    outs_p += [stacked(new_p, i).reshape((depth, bsz, seq) + da_heads) for i in range(2, 4)]
    outs_s = [stacked(new_s, i).reshape((depth, db, dseq) + (sb_heads if i < 2 else da_heads))
              for i in range(4)]
    return (hp.reshape(bsz, seq, d), hs.reshape(db, dseq, d), *outs_p, *outs_s)
```
